```python
import math
import jax, jax.numpy as jnp
from jax import lax
import numpy as np

D_MODEL = 1024
BATCH = 16
SEQ = 2048
DEPTH = 4

GRID_W = 64
CTX_LEN = 256
D_S5 = 512
S5_GROUP = 16
S5_GROUPS = D_S5 // S5_GROUP
S5_STATE = 64
D_CONV = 512
CONV_K = 31
D_FF = 2816
FFN_K = 3
N_BRANCH = 2
OFF_CONV = D_S5
OFF_GATE = D_S5 + 2 * D_CONV
D_IN = OFF_GATE + N_BRANCH * D_MODEL
N_MOD = 6
EPS = 1e-6
DT_MIN = 1e-3
DT_MAX = 1e-1

kernel_name = "hybrid_s5_conformer_prefix_dit"


def rmsnorm(x, g):
    xf = x.astype(jnp.float32)
    y = xf * lax.rsqrt(jnp.mean(xf * xf, axis=-1, keepdims=True) + EPS)
    return (y * g.astype(jnp.float32)).astype(x.dtype)


def layernorm(x, g, b):
    xf = x.astype(jnp.float32)
    mu = jnp.mean(xf, axis=-1, keepdims=True)
    var = jnp.mean(jnp.square(xf - mu), axis=-1, keepdims=True)
    y = (xf - mu) * lax.rsqrt(var + EPS)
    return (y * g.astype(jnp.float32) + b.astype(jnp.float32)).astype(x.dtype)


def modulate(h, shift, scale):
    return h * (1.0 + scale) + shift


def depthwise_conv1d(x, w, b):
    k = w.shape[0]
    y = lax.conv_general_dilated(x, w[:, None, :].astype(x.dtype), window_strides=(1,),
                                 padding=[(k // 2, k // 2)],
                                 dimension_numbers=("NWC", "WIO", "NWC"),
                                 feature_group_count=x.shape[-1])
    return y + b


def depthwise_conv2d(x, w, b):
    y = lax.conv_general_dilated(x, w[:, :, None, :].astype(x.dtype), window_strides=(1, 1),
                                 padding="SAME",
                                 dimension_numbers=("NHWC", "HWIO", "NHWC"),
                                 feature_group_count=x.shape[-1])
    return y + b


def s5_discretise(lam_re, lam_im, log_dt, b_re, b_im):
    dt = jnp.exp(log_dt.astype(jnp.float32))[:, None]
    lr = lam_re.astype(jnp.float32)
    li = lam_im.astype(jnp.float32)
    mag = jnp.exp(dt * lr)
    ab_re = mag * jnp.cos(dt * li)
    ab_im = mag * jnp.sin(dt * li)
    den = lr * lr + li * li
    nr = ab_re - 1.0
    k_re = (nr * lr + ab_im * li) / den
    k_im = (ab_im * lr - nr * li) / den
    br = b_re.astype(jnp.float32)
    bi = b_im.astype(jnp.float32)
    bb_re = k_re[..., None] * br - k_im[..., None] * bi
    bb_im = k_re[..., None] * bi + k_im[..., None] * br
    return ab_re, ab_im, bb_re, bb_im


def _s5_combine(e1, e2):
    a1r, a1i, b1r, b1i = e1
    a2r, a2i, b2r, b2i = e2
    return (a1r * a2r - a1i * a2i,
            a1r * a2i + a1i * a2r,
            a2r * b1r - a2i * b1i + b2r,
            a2r * b1i + a2i * b1r + b2i)


def s5_scan(ab_re, ab_im, bb_re, bb_im, u, h0, reverse):
    uf = u.astype(jnp.float32)
    bu_re = jnp.einsum("blgh,gph->blgp", uf, bb_re)
    bu_im = jnp.einsum("blgh,gph->blgp", uf, bb_im)
    if h0 is not None:
        idx = -1 if reverse else 0
        h0r, h0i = h0
        bu_re = bu_re.at[:, idx].add(ab_re * h0r - ab_im * h0i)
        bu_im = bu_im.at[:, idx].add(ab_re * h0i + ab_im * h0r)
    seq = u.shape[1]
    a_re = jnp.broadcast_to(ab_re, (1, seq) + ab_re.shape)
    a_im = jnp.broadcast_to(ab_im, (1, seq) + ab_im.shape)
    _, _, hr, hi = lax.associative_scan(_s5_combine, (a_re, a_im, bu_re, bu_im),
                                        reverse=reverse, axis=1)
    return hr, hi


def s5_readout(hr, hi, c_re, c_im):
    y = (jnp.einsum("blgp,ghp->blgh", hr, c_re.astype(jnp.float32))
         - jnp.einsum("blgp,ghp->blgh", hi, c_im.astype(jnp.float32)))
    return y.reshape(y.shape[0], y.shape[1], D_S5)


def s5_mixer(u_lat, u_ctx, lam_re, lam_im, log_dt, b_re, b_im, c_re, c_im, d, want_ctx):
    bsz, seq, _ = u_lat.shape
    ul = u_lat.reshape(bsz, seq, S5_GROUPS, S5_GROUP)
    uc = u_ctx.reshape(bsz, u_ctx.shape[1], S5_GROUPS, S5_GROUP)
    dd = d.astype(jnp.float32)
    y_lat = dd * u_lat.astype(jnp.float32)
    y_ctx = dd * u_ctx.astype(jnp.float32) if want_ctx else None
    for direction, reverse in enumerate((False, True)):
        ab_re, ab_im, bb_re, bb_im = s5_discretise(lam_re[direction], lam_im[direction],
                                                   log_dt[direction], b_re[direction], b_im[direction])
        hc_r, hc_i = s5_scan(ab_re, ab_im, bb_re, bb_im, uc, None, reverse)
        edge = 0 if reverse else -1
        h0 = (hc_r[:, edge], hc_i[:, edge])
        hl_r, hl_i = s5_scan(ab_re, ab_im, bb_re, bb_im, ul, h0, reverse)
        y_lat = y_lat + s5_readout(hl_r, hl_i, c_re[direction], c_im[direction])
        if want_ctx:
            y_ctx = y_ctx + s5_readout(hc_r, hc_i, c_re[direction], c_im[direction])
    y_lat = y_lat.astype(u_lat.dtype)
    if want_ctx:
        y_ctx = y_ctx.astype(u_ctx.dtype)
    return y_lat, y_ctx


def s5_glu_out(y, w_glu, w_a):
    g = jax.nn.gelu(y)
    return (g * jax.nn.sigmoid(g @ w_glu)) @ w_a


def conformer_branch(v, dw, dw_b, ln_g, ln_b, w_b):
    a = v[..., :D_CONV] * jax.nn.sigmoid(v[..., D_CONV:])
    a = depthwise_conv1d(a, dw, dw_b)
    a = layernorm(a, ln_g, ln_b)
    return jax.nn.silu(a) @ w_b


def gated_merge(z, ya, yb, w_out):
    ga = z[..., OFF_GATE:OFF_GATE + D_MODEL]
    gb = z[..., OFF_GATE + D_MODEL:OFF_GATE + 2 * D_MODEL]
    return (jax.nn.sigmoid(ga) * ya + jax.nn.sigmoid(gb) * yb) @ w_out


def conv_ffn(h, w_up, dw, dw_b, w_down, rows):
    up = h @ w_up
    gate, val = up[..., :D_FF], up[..., D_FF:]
    bsz, seq, _ = h.shape
    if rows > 0:
        gate = depthwise_conv2d(gate.reshape(bsz, rows, GRID_W, D_FF), dw, dw_b).reshape(bsz, seq, D_FF)
    else:
        gate = depthwise_conv1d(gate, dw[FFN_K // 2], dw_b)
    return (jax.nn.silu(gate) * val) @ w_down


def setup_inputs(seed: int = 0) -> dict:
    key = jax.random.key(seed)
    ks = list(jax.random.split(key, 32))

    def nrm(shape, s):
        return jax.random.normal(ks.pop(), shape, jnp.float32) * s

    L = DEPTH
    G, P, H = S5_GROUPS, S5_STATE, S5_GROUP
    n_idx = jnp.arange(P, dtype=jnp.float32)
    inp = {}
    inp["x"] = nrm((BATCH, SEQ, D_MODEL), 1.0)
    inp["c"] = nrm((BATCH, D_MODEL), 1.0)
    inp["ctx"] = nrm((BATCH, CTX_LEN, D_MODEL), 1.0)
    inp["c_ctx"] = nrm((D_MODEL,), 1.0)
    inp["ada_w"] = nrm((L, D_MODEL, N_MOD * D_MODEL), 0.5 * D_MODEL ** -0.5)
    inp["ada_b"] = nrm((L, N_MOD * D_MODEL), 0.02)
    inp["norm1_g"] = 1.0 + nrm((L, D_MODEL), 0.02)
    inp["w_in"] = nrm((L, D_MODEL, D_IN), D_MODEL ** -0.5)
    inp["s5_lam_re"] = -0.5 + nrm((L, 2, G, P), 0.01)
    inp["s5_lam_im"] = jnp.pi * n_idx + nrm((L, 2, G, P), 0.01)
    inp["s5_log_dt"] = jax.random.uniform(ks.pop(), (L, 2, G), jnp.float32,
                                          math.log(DT_MIN), math.log(DT_MAX))
    inp["s5_b_re"] = nrm((L, 2, G, P, H), (2 * H) ** -0.5)
    inp["s5_b_im"] = nrm((L, 2, G, P, H), (2 * H) ** -0.5)
    inp["s5_c_re"] = nrm((L, 2, G, H, P), P ** -0.5)
    inp["s5_c_im"] = nrm((L, 2, G, H, P), P ** -0.5)
    inp["s5_d"] = nrm((L, D_S5), 1.0)
    inp["w_glu"] = nrm((L, D_S5, D_S5), D_S5 ** -0.5)
    inp["w_a"] = nrm((L, D_S5, D_MODEL), D_S5 ** -0.5)
    inp["conv_dw"] = nrm((L, CONV_K, D_CONV), CONV_K ** -0.5)
    inp["conv_dw_b"] = nrm((L, D_CONV), 0.01)
    inp["conv_ln_g"] = 1.0 + nrm((L, D_CONV), 0.02)
    inp["conv_ln_b"] = nrm((L, D_CONV), 0.01)
    inp["w_b"] = nrm((L, D_CONV, D_MODEL), D_CONV ** -0.5)
    inp["w_out"] = nrm((L, D_MODEL, D_MODEL), D_MODEL ** -0.5)
    inp["norm2_g"] = 1.0 + nrm((L, D_MODEL), 0.02)
    inp["ffn_w_up"] = nrm((L, D_MODEL, 2 * D_FF), D_MODEL ** -0.5)
    inp["ffn_dw"] = nrm((L, FFN_K, FFN_K, D_FF), 1.0 / FFN_K)
    inp["ffn_dw_b"] = nrm((L, D_FF), 0.01)
    inp["ffn_w_down"] = nrm((L, D_FF, D_MODEL), D_FF ** -0.5)
    inp["final_g"] = 1.0 + nrm((D_MODEL,), 0.02)
    return inp


def reference(x, c, ctx, c_ctx, ada_w, ada_b, norm1_g, w_in, s5_lam_re, s5_lam_im, s5_log_dt,
              s5_b_re, s5_b_im, s5_c_re, s5_c_im, s5_d, w_glu, w_a, conv_dw, conv_dw_b,
              conv_ln_g, conv_ln_b, w_b, w_out, norm2_g, ffn_w_up, ffn_dw, ffn_dw_b,
              ffn_w_down, final_g):
    rows = x.shape[1] // GRID_W
    silu_c = jax.nn.silu(c)
    silu_cc = jax.nn.silu(c_ctx)
    h_lat = x
    h_ctx = ctx
    for i in range(DEPTH):
        last = i == DEPTH - 1
        mod_l = (silu_c @ ada_w[i] + ada_b[i])[:, None, :]
        sh1, sc1, g1, sh2, sc2, g2 = jnp.split(mod_l, N_MOD, axis=-1)
        mod_c = silu_cc @ ada_w[i] + ada_b[i]
        csh1, csc1, cg1, csh2, csc2, cg2 = jnp.split(mod_c, N_MOD, axis=-1)

        nl = modulate(rmsnorm(h_lat, norm1_g[i]), sh1, sc1)
        nc = modulate(rmsnorm(h_ctx, norm1_g[i]), csh1, csc1)
        zl = nl @ w_in[i]
        zc = nc @ (w_in[i][:, :D_S5] if last else w_in[i])
        ys_l, ys_c = s5_mixer(zl[..., :D_S5], zc[..., :D_S5], s5_lam_re[i], s5_lam_im[i],
                              s5_log_dt[i], s5_b_re[i], s5_b_im[i], s5_c_re[i], s5_c_im[i],
                              s5_d[i], not last)
        ya_l = s5_glu_out(ys_l, w_glu[i], w_a[i])
        yb_l = conformer_branch(zl[..., OFF_CONV:OFF_GATE], conv_dw[i], conv_dw_b[i],
                                conv_ln_g[i], conv_ln_b[i], w_b[i])
        h_lat = h_lat + g1 * gated_merge(zl, ya_l, yb_l, w_out[i])
        if not last:
            ya_c = s5_glu_out(ys_c, w_glu[i], w_a[i])
            yb_c = conformer_branch(zc[..., OFF_CONV:OFF_GATE], conv_dw[i], conv_dw_b[i],
                                    conv_ln_g[i], conv_ln_b[i], w_b[i])
            h_ctx = h_ctx + cg1 * gated_merge(zc, ya_c, yb_c, w_out[i])

        nl = modulate(rmsnorm(h_lat, norm2_g[i]), sh2, sc2)
        h_lat = h_lat + g2 * conv_ffn(nl, ffn_w_up[i], ffn_dw[i], ffn_dw_b[i], ffn_w_down[i], rows)
        if not last:
            nc = modulate(rmsnorm(h_ctx, norm2_g[i]), csh2, csc2)
            h_ctx = h_ctx + cg2 * conv_ffn(nc, ffn_w_up[i], ffn_dw[i], ffn_dw_b[i], ffn_w_down[i], 0)
    return rmsnorm(h_lat, final_g)
```

```python
import functools

import jax
import jax.numpy as jnp
from jax import lax
from jax.experimental import pallas as pl
from jax.experimental.pallas import tpu as pltpu

GRID_W = 64
S5_GROUP = 16
S5_STATE = 64
N_MOD = 6
EPS = 1e-6
LANES = 128
S5_BLOCK_GROUPS = LANES // S5_GROUP
MIX_STEPS = 32
CONV_HALO_STEPS = 16
FFN_BLOCK = 256
VMEM_LIMIT = 56 * 1024 * 1024

BF16 = jnp.bfloat16
F32 = jnp.float32


def _dot(a, b):
    return jnp.dot(a, b, preferred_element_type=F32)


def _split_bf16(a):
    hi = a.astype(BF16)
    lo = (a - hi.astype(F32)).astype(BF16)
    return hi, lo


def _dot_3pass(a, b):
    a_hi, a_lo = _split_bf16(a)
    b_hi, b_lo = _split_bf16(b)
    return _dot(a_hi, b_hi) + (_dot(a_lo, b_hi) + _dot(a_hi, b_lo))


def _params(*sem):
    return pltpu.CompilerParams(dimension_semantics=sem, vmem_limit_bytes=VMEM_LIMIT)


def _norm_modulate(x, gain, mod_ref, batch, k_shift, k_scale):
    rows, d = x.shape
    ms = jnp.mean(x * x, axis=-1, keepdims=True)
    n = (x * lax.rsqrt(ms + EPS)) * gain
    shift = mod_ref[:, k_shift * d:(k_shift + 1) * d]
    scale = mod_ref[:, k_scale * d:(k_scale + 1) * d]
    n3 = n.reshape(rows // batch, batch, d)
    out = n3 * (1.0 + scale)[None] + shift[None]
    return out.reshape(rows, d)


def _gated_residual(x, y, mod_ref, batch, k_gate):
    rows, d = x.shape
    gate = mod_ref[:, k_gate * d:(k_gate + 1) * d]
    out = x.reshape(rows // batch, batch, d) + gate[None] * y.reshape(rows // batch, batch, d)
    return out.reshape(rows, d)


def _mod_kernel(c_ref, w_ref, b_ref, o_ref):
    c = c_ref[...]
    s = c * jax.nn.sigmoid(c)
    o_ref[...] = _dot_3pass(s, w_ref[...]) + b_ref[...]


def _modulation(cc, ada_w, ada_b):
    depth, d, n = ada_w.shape
    rows = cc.shape[0]
    tn = d
    return pl.pallas_call(
        _mod_kernel,
        grid=(depth, n // tn),
        in_specs=[pl.BlockSpec((rows, d), lambda l, j: (0, 0)),
                  pl.BlockSpec((None, d, tn), lambda l, j: (l, 0, j)),
                  pl.BlockSpec((None, 1, tn), lambda l, j: (l, 0, j))],
        out_specs=pl.BlockSpec((None, rows, tn), lambda l, j: (l, 0, j)),
        out_shape=jax.ShapeDtypeStruct((depth, rows, n), F32),
        compiler_params=_params("arbitrary", "arbitrary"),
        name="adaln_mod",
    )(cc, ada_w, ada_b.reshape(depth, 1, n))


def _disc_kernel(lr_ref, li_ref, ldt_ref, br_ref, bi_ref, ar_ref, ai_ref, bbr_ref, bbi_ref):
    dt = jnp.exp(ldt_ref[...])
    lr = lr_ref[...]
    li = li_ref[...]
    mag = jnp.exp(dt * lr)
    ab_re = mag * jnp.cos(dt * li)
    ab_im = mag * jnp.sin(dt * li)
    den = lr * lr + li * li
    nr = ab_re - 1.0
    k_re = (nr * lr + ab_im * li) / den
    k_im = (ab_im * lr - nr * li) / den
    br = br_ref[...]
    bi = bi_ref[...]
    ar_ref[...] = ab_re
    ai_ref[...] = ab_im
    bbr_ref[...] = k_re * br - k_im * bi
    bbi_ref[...] = k_re * bi + k_im * br


def _discretise(lam_re, lam_im, log_dt, b_re, b_im):
    lead = lam_re.shape[:-1]
    p = lam_re.shape[-1]
    h = b_re.shape[-1]
    r = 1
    for s in lead:
        r *= s
    lr = lam_re.reshape(r, 1, p)
    li = lam_im.reshape(r, 1, p)
    ldt = jnp.broadcast_to(log_dt.reshape(r, 1, 1), (r, 1, p))
    br = jnp.swapaxes(b_re, -1, -2).reshape(r, h, p)
    bi = jnp.swapaxes(b_im, -1, -2).reshape(r, h, p)
    full = lambda shape: pl.BlockSpec(shape, lambda i: (0,) * len(shape))
    ar, ai, bbr, bbi = pl.pallas_call(
        _disc_kernel,
        grid=(1,),
        in_specs=[full((r, 1, p)), full((r, 1, p)), full((r, 1, p)), full((r, h, p)), full((r, h, p))],
        out_specs=[full((r, 1, p)), full((r, 1, p)), full((r, h, p)), full((r, h, p))],
        out_shape=[jax.ShapeDtypeStruct((r, 1, p), F32), jax.ShapeDtypeStruct((r, 1, p), F32),
                   jax.ShapeDtypeStruct((r, h, p), F32), jax.ShapeDtypeStruct((r, h, p), F32)],
        compiler_params=_params("arbitrary"),
        name="s5_discretise",
    )(lr, li, ldt, br, bi)
    return (ar.reshape(lead + (p,)), ai.reshape(lead + (p,)),
            bbr.reshape(lead + (h, p)), bbi.reshape(lead + (h, p)))


def _block_diag(m):
    g = m.shape[-3]
    eye = jnp.eye(g, dtype=m.dtype)
    out = m[..., :, :, None, :] * eye[:, None, :, None]
    return out.reshape(m.shape[:-3] + (g * m.shape[-2], g * m.shape[-1]))


def _s5_operands(ar, ai, bbr, bbi, c_re, c_im):
    lead = ar.shape[:-2]
    g, p = ar.shape[-2:]
    h = bbr.shape[-2]
    nb = g // S5_BLOCK_GROUPS
    bg = S5_BLOCK_GROUPS
    blk = lambda m: m.reshape(lead + (nb, bg) + m.shape[-2:])
    b_in = jnp.concatenate([_block_diag(blk(bbr)), _block_diag(blk(bbi))], axis=-1)
    ct = lambda m: jnp.swapaxes(m, -1, -2)
    c_out = jnp.concatenate([_block_diag(blk(ct(c_re))), -_block_diag(blk(ct(c_im)))], axis=-2)
    a_re = ar.reshape(lead + (nb, 1, bg * p))
    a_im = ai.reshape(lead + (nb, 1, bg * p))
    return b_in.astype(BF16), c_out.astype(BF16), a_re, a_im


def _inproj_kernel(h_ref, mod_ref, g_ref, w_ref, u_ref, a_ref, sg_ref, *, batch, d_s5, d_conv):
    nl = _norm_modulate(h_ref[...], g_ref[...], mod_ref, batch, 0, 1).astype(BF16)
    u_ref[...] = _dot(nl, w_ref[:, :d_s5])
    off = d_s5
    v1 = _dot(nl, w_ref[:, off:off + d_conv])
    v2 = _dot(nl, w_ref[:, off + d_conv:off + 2 * d_conv])
    a_ref[...] = v1 * jax.nn.sigmoid(v2)
    off = d_s5 + 2 * d_conv
    d = h_ref.shape[1]
    for k in range(2):
        sg_ref[:, k * d:(k + 1) * d] = jax.nn.sigmoid(_dot(nl, w_ref[:, off + k * d:off + (k + 1) * d]))


def _inproj(h, mod, gain, w_in, *, batch, n_ctx_tiles, d_s5, d_conv):
    rows, d = h.shape
    tr = MIX_STEPS * batch
    d_in = w_in.shape[1]
    kind = lambda i: (jnp.where(i >= n_ctx_tiles, 1, 0), 0, 0)
    return pl.pallas_call(
        functools.partial(_inproj_kernel, batch=batch, d_s5=d_s5, d_conv=d_conv),
        grid=(rows // tr,),
        in_specs=[pl.BlockSpec((tr, d), lambda i: (i, 0)),
                  pl.BlockSpec((None, batch, N_MOD * d), kind),
                  pl.BlockSpec((1, d), lambda i: (0, 0)),
                  pl.BlockSpec((d, d_in), lambda i: (0, 0))],
        out_specs=[pl.BlockSpec((tr, d_s5), lambda i: (i, 0)),
                   pl.BlockSpec((tr, d_conv), lambda i: (i, 0)),
                   pl.BlockSpec((tr, 2 * d), lambda i: (i, 0))],
        out_shape=[jax.ShapeDtypeStruct((rows, d_s5), F32),
                   jax.ShapeDtypeStruct((rows, d_conv), F32),
                   jax.ShapeDtypeStruct((rows, 2 * d), F32)],
        compiler_params=_params("arbitrary"),
        name="mix_inproj",
    )(h, mod, gain, w_in)


def _s5_kernel(uf_ref, ub_ref, bin_ref, cout_ref, are_ref, aim_ref, yf_ref, yb_ref,
               buf_f, buf_b, hf, hb, *, batch, steps):
    k = pl.program_id(1)
    ns = are_ref.shape[-1]

    @pl.when(k == 0)
    def _():
        hf[...] = jnp.zeros_like(hf)
        hb[...] = jnp.zeros_like(hb)

    buf_f[...] = _dot(uf_ref[...].astype(BF16), bin_ref[0])
    buf_b[...] = _dot(ub_ref[...].astype(BF16), bin_ref[1])

    def scan(buf, carry, direction, reverse):
        a_re = jnp.broadcast_to(are_ref[direction], (batch, ns))
        a_im = jnp.broadcast_to(aim_ref[direction], (batch, ns))

        def step(s, state):
            h_re, h_im = state
            t = (steps - 1 - s) if reverse else s
            r0 = pl.multiple_of(t * batch, batch)
            n_re = a_re * h_re - a_im * h_im + buf[pl.ds(r0, batch), 0:ns]
            n_im = a_re * h_im + a_im * h_re + buf[pl.ds(r0, batch), ns:2 * ns]
            buf[pl.ds(r0, batch), 0:ns] = n_re
            buf[pl.ds(r0, batch), ns:2 * ns] = n_im
            return n_re, n_im

        h_re, h_im = lax.fori_loop(0, steps, step, (carry[:, 0:ns], carry[:, ns:2 * ns]))
        carry[:, 0:ns] = h_re
        carry[:, ns:2 * ns] = h_im

    scan(buf_f, hf, 0, False)
    scan(buf_b, hb, 1, True)
    yf_ref[...] = _dot(buf_f[...].astype(BF16), cout_ref[0])
    yb_ref[...] = _dot(buf_b[...].astype(BF16), cout_ref[1])


def _s5_scan(u, b_in, c_out, a_re, a_im, *, batch, n_ctx_tiles):
    rows, d_s5 = u.shape
    nb = b_in.shape[1]
    ns = a_re.shape[-1]
    tr = MIX_STEPS * batch
    nt = rows // tr

    def rev(k):
        return jnp.where(k < n_ctx_tiles, n_ctx_tiles - 1 - k, nt - 1 + n_ctx_tiles - k)

    return pl.pallas_call(
        functools.partial(_s5_kernel, batch=batch, steps=MIX_STEPS),
        grid=(nb, nt),
        in_specs=[pl.BlockSpec((tr, LANES), lambda j, k: (k, j)),
                  pl.BlockSpec((tr, LANES), lambda j, k: (rev(k), j)),
                  pl.BlockSpec((2, None, LANES, 2 * ns), lambda j, k: (0, j, 0, 0)),
                  pl.BlockSpec((2, None, 2 * ns, LANES), lambda j, k: (0, j, 0, 0)),
                  pl.BlockSpec((2, None, 1, ns), lambda j, k: (0, j, 0, 0)),
                  pl.BlockSpec((2, None, 1, ns), lambda j, k: (0, j, 0, 0))],
        out_specs=[pl.BlockSpec((tr, LANES), lambda j, k: (k, j)),
                   pl.BlockSpec((tr, LANES), lambda j, k: (rev(k), j))],
        out_shape=[jax.ShapeDtypeStruct((rows, d_s5), F32), jax.ShapeDtypeStruct((rows, d_s5), F32)],
        scratch_shapes=[pltpu.VMEM((tr, 2 * ns), F32), pltpu.VMEM((tr, 2 * ns), F32),
                        pltpu.VMEM((batch, 2 * ns), F32), pltpu.VMEM((batch, 2 * ns), F32)],
        compiler_params=_params("arbitrary", "arbitrary"),
        name="s5_scan",
    )(u, u, b_in, c_out, a_re, a_im)


def _mixout_kernel(h_ref, mod_ref, u_ref, yf_ref, yb_ref, ac_ref, ap_ref, an_ref, sg_ref,
                   sd_ref, wglu_ref, wa_ref, dw_ref, dwb_ref, lng_ref, lnb_ref, wb_ref, wout_ref,
                   o_ref, aext, conv, *, batch, n_ctx_tiles, n_tiles):
    i = pl.program_id(0)
    tr, d = h_ref.shape
    halo = ap_ref.shape[0]
    taps = dw_ref.shape[0]

    ys = sd_ref[...] * u_ref[...] + yf_ref[...] + yb_ref[...]
    g = jax.nn.gelu(ys)
    ya = _dot((g * jax.nn.sigmoid(_dot(g.astype(BF16), wglu_ref[...]))).astype(BF16), wa_ref[...])

    first = jnp.logical_or(i == 0, i == n_ctx_tiles)
    last = jnp.logical_or(i == n_ctx_tiles - 1, i == n_tiles - 1)
    aext[0:halo] = jnp.where(first, 0.0, ap_ref[...])
    aext[halo:halo + tr] = ac_ref[...]
    aext[halo + tr:2 * halo + tr] = jnp.where(last, 0.0, an_ref[...])
    chunk = 2 * batch

    def conv_chunk(c, _):
        r0 = pl.multiple_of(c * chunk, chunk)
        acc = jnp.broadcast_to(dwb_ref[...], (chunk, dwb_ref.shape[1]))
        for k in range(taps):
            off = halo + (k - taps // 2) * batch
            acc = acc + dw_ref[k:k + 1, :] * aext[pl.ds(r0 + off, chunk), :]
        conv[pl.ds(r0, chunk), :] = acc
        return 0

    lax.fori_loop(0, tr // chunk, conv_chunk, 0)
    a = conv[...]
    mu = jnp.mean(a, axis=-1, keepdims=True)
    ac = a - mu
    var = jnp.mean(ac * ac, axis=-1, keepdims=True)
    ln = (ac * lax.rsqrt(var + EPS)) * lng_ref[...] + lnb_ref[...]
    yb = _dot((ln * jax.nn.sigmoid(ln)).astype(BF16), wb_ref[...])

    merged = sg_ref[:, 0:d] * ya + sg_ref[:, d:2 * d] * yb
    y = _dot(merged.astype(BF16), wout_ref[...])
    o_ref[...] = _gated_residual(h_ref[...], y, mod_ref, batch, 2)


def _mixout(h, mod, u, yf, yb, a, sg, s5_d, w_glu, w_a, conv_dw, conv_dw_b, ln_g, ln_b, w_b, w_out,
            *, batch, n_ctx_tiles):
    rows, d = h.shape
    d_s5 = u.shape[1]
    d_conv = a.shape[1]
    tr = MIX_STEPS * batch
    halo = CONV_HALO_STEPS * batch
    ratio = tr // halo
    nt = rows // tr
    n_halo_blocks = rows // halo
    taps = conv_dw.shape[0]
    assert taps // 2 <= CONV_HALO_STEPS
    kind = lambda i: (jnp.where(i >= n_ctx_tiles, 1, 0), 0, 0)
    row_tile = lambda w: pl.BlockSpec((tr, w), lambda i: (i, 0))
    const = lambda s: pl.BlockSpec(s, lambda i: (0,) * len(s))
    return pl.pallas_call(
        functools.partial(_mixout_kernel, batch=batch, n_ctx_tiles=n_ctx_tiles, n_tiles=nt),
        grid=(nt,),
        in_specs=[row_tile(d),
                  pl.BlockSpec((None, batch, N_MOD * d), kind),
                  row_tile(d_s5), row_tile(d_s5), row_tile(d_s5),
                  row_tile(d_conv),
                  pl.BlockSpec((halo, d_conv), lambda i: (jnp.maximum(i * ratio - 1, 0), 0)),
                  pl.BlockSpec((halo, d_conv), lambda i: (jnp.minimum((i + 1) * ratio, n_halo_blocks - 1), 0)),
                  row_tile(2 * d),
                  const((1, d_s5)), const((d_s5, d_s5)), const((d_s5, d)),
                  const((taps, d_conv)), const((1, d_conv)), const((1, d_conv)), const((1, d_conv)),
                  const((d_conv, d)), const((d, d))],
        out_specs=row_tile(d),
        out_shape=jax.ShapeDtypeStruct((rows, d), F32),
        scratch_shapes=[pltpu.VMEM((tr + 2 * halo, d_conv), F32), pltpu.VMEM((tr, d_conv), F32)],
        compiler_params=_params("arbitrary"),
        name="mix_out",
    )(h, mod, u, yf, yb, a, a, a, sg, s5_d, w_glu, w_a, conv_dw, conv_dw_b, ln_g, ln_b, w_b, w_out)


def _ffn_gate_kernel(h_ref, mod_ref, g_ref, w_ref, o_ref, *, batch):
    nl = _norm_modulate(h_ref[...], g_ref[...], mod_ref, batch, 3, 4).astype(BF16)
    d_ff = o_ref.shape[1]
    for c in range(d_ff // FFN_BLOCK):
        cols = slice(c * FFN_BLOCK, (c + 1) * FFN_BLOCK)
        o_ref[:, cols] = _dot(nl, w_ref[:, cols]).astype(BF16)


def _ffn_gate(h, mod, gain, w_up, *, batch, n_ctx_tiles, d_ff):
    rows, d = h.shape
    tr = GRID_W * batch
    kind = lambda i: (jnp.where(i >= n_ctx_tiles, 1, 0), 0, 0)
    return pl.pallas_call(
        functools.partial(_ffn_gate_kernel, batch=batch),
        grid=(rows // tr,),
        in_specs=[pl.BlockSpec((tr, d), lambda i: (i, 0)),
                  pl.BlockSpec((None, batch, N_MOD * d), kind),
                  pl.BlockSpec((1, d), lambda i: (0, 0)),
                  pl.BlockSpec((d, d_ff), lambda i: (0, 0))],
        out_specs=pl.BlockSpec((tr, d_ff), lambda i: (i, 0)),
        out_shape=jax.ShapeDtypeStruct((rows, d_ff), BF16),
        compiler_params=_params("arbitrary"),
        name="ffn_gate",
    )(h, mod, gain, w_up)


def _ffn_kernel(h_ref, mod_ref, g_ref, gp_ref, gc_ref, gn_ref, wv_ref, wd_ref, dw_ref, db_ref,
                o_ref, nl_ref, pad, *, batch, n_ctx_tiles, n_tiles):
    i = pl.program_id(0)
    f = pl.program_id(1)
    nf = pl.num_programs(1)
    tr, d = h_ref.shape
    fw = gc_ref.shape[1]
    ksz = 3

    @pl.when(f == 0)
    def _():
        nl_ref[...] = _norm_modulate(h_ref[...], g_ref[...], mod_ref, batch, 3, 4).astype(BF16)

    val = _dot(nl_ref[...], wv_ref[...])

    is_ctx = i < n_ctx_tiles
    row = i - n_ctx_tiles
    up_ok = jnp.logical_and(jnp.logical_not(is_ctx), row >= 1)
    down_ok = jnp.logical_and(jnp.logical_not(is_ctx), i <= n_tiles - 2)
    ctx_lo = jnp.logical_and(is_ctx, i >= 1)
    ctx_hi = jnp.logical_and(is_ctx, i <= n_ctx_tiles - 2)
    zeros = jnp.zeros((batch, fw), F32)
    srcs = (gp_ref, gc_ref, gn_ref)
    for s in range(ksz):
        pad[s, batch:batch + tr, :] = srcs[s][...].astype(F32)
    pad[0, 0:batch, :] = zeros
    pad[0, batch + tr:2 * batch + tr, :] = zeros
    pad[2, 0:batch, :] = zeros
    pad[2, batch + tr:2 * batch + tr, :] = zeros
    pad[1, 0:batch, :] = jnp.where(ctx_lo, gp_ref[tr - batch:tr, :].astype(F32), 0.0)
    pad[1, batch + tr:2 * batch + tr, :] = jnp.where(ctx_hi, gn_ref[0:batch, :].astype(F32), 0.0)
    row_ok = (up_ok, True, down_ok)
    acc = jnp.broadcast_to(db_ref[...], (tr, fw))
    for s in range(ksz):
        for dwi in range(ksz):
            w = dw_ref[s * ksz + dwi:s * ksz + dwi + 1, :]
            if s != 1:
                w = jnp.where(row_ok[s], w, 0.0)
            acc = acc + w * pad[s, dwi * batch:dwi * batch + tr, :]
    act = (acc * jax.nn.sigmoid(acc)) * val
    y = _dot(act.astype(BF16), wd_ref[...])

    @pl.when(f == 0)
    def _():
        o_ref[...] = y

    @pl.when(f > 0)
    def _():
        o_ref[...] += y

    @pl.when(f == nf - 1)
    def _():
        o_ref[...] = _gated_residual(h_ref[...], o_ref[...], mod_ref, batch, 5)


def _ffn(h, mod, gain, gate, w_up, w_down, dw, dw_b, *, batch, n_ctx_tiles, d_ff):
    rows, d = h.shape
    tr = GRID_W * batch
    nt = rows // tr
    nf = d_ff // FFN_BLOCK
    kind = lambda i, f: (jnp.where(i >= n_ctx_tiles, 1, 0), 0, 0)
    return pl.pallas_call(
        functools.partial(_ffn_kernel, batch=batch, n_ctx_tiles=n_ctx_tiles, n_tiles=nt),
        grid=(nt, nf),
        in_specs=[pl.BlockSpec((tr, d), lambda i, f: (i, 0)),
                  pl.BlockSpec((None, batch, N_MOD * d), kind),
                  pl.BlockSpec((1, d), lambda i, f: (0, 0)),
                  pl.BlockSpec((tr, FFN_BLOCK), lambda i, f: (jnp.maximum(i - 1, 0), f)),
                  pl.BlockSpec((tr, FFN_BLOCK), lambda i, f: (i, f)),
                  pl.BlockSpec((tr, FFN_BLOCK), lambda i, f: (jnp.minimum(i + 1, nt - 1), f)),
                  pl.BlockSpec((d, FFN_BLOCK), lambda i, f: (0, nf + f)),
                  pl.BlockSpec((FFN_BLOCK, d), lambda i, f: (f, 0)),
                  pl.BlockSpec((9, FFN_BLOCK), lambda i, f: (0, f)),
                  pl.BlockSpec((1, FFN_BLOCK), lambda i, f: (0, f))],
        out_specs=pl.BlockSpec((tr, d), lambda i, f: (i, 0)),
        out_shape=jax.ShapeDtypeStruct((rows, d), F32),
        scratch_shapes=[pltpu.VMEM((tr, d), BF16), pltpu.VMEM((3, tr + 2 * batch, FFN_BLOCK), F32)],
        compiler_params=_params("arbitrary", "arbitrary"),
        name="ffn_main",
    )(h, mod, gain, gate, gate, gate, w_up, w_down, dw, dw_b)


def _final_norm_kernel(h_ref, g_ref, o_ref):
    x = h_ref[...]
    ms = jnp.mean(x * x, axis=-1, keepdims=True)
    o_ref[...] = (x * lax.rsqrt(ms + EPS)) * g_ref[...]


def _final_norm(h, gain, *, first_row, out_rows):
    d = h.shape[1]
    tr = 1024
    off = first_row // tr
    return pl.pallas_call(
        _final_norm_kernel,
        grid=(out_rows // tr,),
        in_specs=[pl.BlockSpec((tr, d), lambda i: (i + off, 0)),
                  pl.BlockSpec((1, d), lambda i: (0, 0))],
        out_specs=pl.BlockSpec((tr, d), lambda i: (i, 0)),
        out_shape=jax.ShapeDtypeStruct((out_rows, d), F32),
        compiler_params=_params("arbitrary"),
        name="final_norm",
    )(h, gain)


def kernel(x, c, ctx, c_ctx, ada_w, ada_b, norm1_g, w_in, s5_lam_re, s5_lam_im, s5_log_dt, s5_b_re, s5_b_im, s5_c_re, s5_c_im, s5_d, w_glu, w_a, conv_dw, conv_dw_b, conv_ln_g, conv_ln_b, w_b, w_out, norm2_g, ffn_w_up, ffn_dw, ffn_dw_b, ffn_w_down, final_g):
    batch, seq, d = x.shape
    ctx_len = ctx.shape[1]
    depth = ada_w.shape[0]
    d_s5 = w_glu.shape[1]
    d_conv = conv_dw.shape[2]
    d_ff = ffn_dw.shape[-1]
    assert batch % 16 == 0 and seq % GRID_W == 0 and ctx_len % GRID_W == 0
    assert d_s5 % LANES == 0 and d_ff % FFN_BLOCK == 0
    mix_ctx_tiles = ctx_len // MIX_STEPS
    ffn_ctx_tiles = ctx_len // GRID_W

    h = jnp.concatenate([jnp.swapaxes(ctx, 0, 1), jnp.swapaxes(x, 0, 1)], axis=0).reshape(-1, d)

    cc = jnp.concatenate([jnp.broadcast_to(c_ctx[None], (batch, d)), c], axis=0)
    mods = _modulation(cc, ada_w, ada_b).reshape(depth, 2, batch, N_MOD * d)

    ar, ai, bbr, bbi = _discretise(s5_lam_re, s5_lam_im, s5_log_dt, s5_b_re, s5_b_im)
    b_in, c_out, a_re, a_im = _s5_operands(ar, ai, bbr, bbi, s5_c_re, s5_c_im)

    row = lambda v: v.reshape(1, -1)
    for i in range(depth):
        mod = mods[i]
        u, a, sg = _inproj(h, mod, row(norm1_g[i]), w_in[i].astype(BF16), batch=batch,
                           n_ctx_tiles=mix_ctx_tiles, d_s5=d_s5, d_conv=d_conv)
        yf, yb = _s5_scan(u, b_in[i], c_out[i], a_re[i], a_im[i], batch=batch, n_ctx_tiles=mix_ctx_tiles)
        h = _mixout(h, mod, u, yf, yb, a, sg, row(s5_d[i]), w_glu[i].astype(BF16), w_a[i].astype(BF16),
                    conv_dw[i], row(conv_dw_b[i]), row(conv_ln_g[i]), row(conv_ln_b[i]),
                    w_b[i].astype(BF16), w_out[i].astype(BF16), batch=batch, n_ctx_tiles=mix_ctx_tiles)
        w_up = ffn_w_up[i].astype(BF16)
        gate = _ffn_gate(h, mod, row(norm2_g[i]), w_up, batch=batch, n_ctx_tiles=ffn_ctx_tiles, d_ff=d_ff)
        h = _ffn(h, mod, row(norm2_g[i]), gate, w_up, ffn_w_down[i].astype(BF16),
                 ffn_dw[i].reshape(9, d_ff), row(ffn_dw_b[i]), batch=batch, n_ctx_tiles=ffn_ctx_tiles, d_ff=d_ff)

    out = _final_norm(h, row(final_g), first_row=ctx_len * batch, out_rows=seq * batch)
    return jnp.swapaxes(out.reshape(seq, batch, d), 0, 1)
```

```python
import functools

import jax
import jax.numpy as jnp
from jax import lax
from jax.experimental import pallas as pl
from jax.experimental.pallas import tpu as pltpu

GRID_W = 64
S5_GROUP = 16
S5_STATE = 64
N_MOD = 6
EPS = 1e-6
LANES = 128
S5_BLOCK_GROUPS = LANES // S5_GROUP
MIX_STEPS = 32
CONV_HALO_STEPS = 16
FFN_BLOCK = 256
FFN_ROW_BLOCK = 256
MXU_N = 256
VMEM_LIMIT = 56 * 1024 * 1024
FFN_VMEM_LIMIT = 60 * 1024 * 1024

BF16 = jnp.bfloat16
F32 = jnp.float32


def _dot(a, b):
    return jnp.dot(a, b, preferred_element_type=F32)


def _split_bf16(a):
    hi = a.astype(BF16)
    lo = (a - hi.astype(F32)).astype(BF16)
    return hi, lo


def _dot_3pass(a, b):
    a_hi, a_lo = _split_bf16(a)
    b_hi, b_lo = _split_bf16(b)
    return _dot(a_hi, b_hi) + (_dot(a_lo, b_hi) + _dot(a_hi, b_lo))


def _params(*sem):
    return pltpu.CompilerParams(dimension_semantics=sem, vmem_limit_bytes=VMEM_LIMIT)


def _norm_modulate(x, gain, mod_ref, batch, k_shift, k_scale):
    rows, d = x.shape
    ms = jnp.mean(x * x, axis=-1, keepdims=True)
    n = (x * lax.rsqrt(ms + EPS)) * gain
    shift = mod_ref[:, k_shift * d:(k_shift + 1) * d]
    scale = mod_ref[:, k_scale * d:(k_scale + 1) * d]
    n3 = n.reshape(rows // batch, batch, d)
    out = n3 * (1.0 + scale)[None] + shift[None]
    return out.reshape(rows, d)


def _gated_residual(x, y, mod_ref, batch, k_gate):
    rows, d = x.shape
    gate = mod_ref[:, k_gate * d:(k_gate + 1) * d]
    out = x.reshape(rows // batch, batch, d) + gate[None] * y.reshape(rows // batch, batch, d)
    return out.reshape(rows, d)


def _mod_kernel(c_ref, w_ref, b_ref, o_ref):
    c = c_ref[...]
    s = c * jax.nn.sigmoid(c)
    o_ref[...] = _dot_3pass(s, w_ref[...]) + b_ref[...]


def _modulation(cc, ada_w, ada_b):
    depth, d, n = ada_w.shape
    rows = cc.shape[0]
    tn = d
    return pl.pallas_call(
        _mod_kernel,
        grid=(depth, n // tn),
        in_specs=[pl.BlockSpec((rows, d), lambda l, j: (0, 0)),
                  pl.BlockSpec((None, d, tn), lambda l, j: (l, 0, j)),
                  pl.BlockSpec((None, 1, tn), lambda l, j: (l, 0, j))],
        out_specs=pl.BlockSpec((None, rows, tn), lambda l, j: (l, 0, j)),
        out_shape=jax.ShapeDtypeStruct((depth, rows, n), F32),
        compiler_params=_params("arbitrary", "arbitrary"),
        name="adaln_mod",
    )(cc, ada_w, ada_b.reshape(depth, 1, n))


def _disc_kernel(lr_ref, li_ref, ldt_ref, br_ref, bi_ref, ar_ref, ai_ref, bbr_ref, bbi_ref):
    dt = jnp.exp(ldt_ref[...])
    lr = lr_ref[...]
    li = li_ref[...]
    mag = jnp.exp(dt * lr)
    ab_re = mag * jnp.cos(dt * li)
    ab_im = mag * jnp.sin(dt * li)
    den = lr * lr + li * li
    nr = ab_re - 1.0
    k_re = (nr * lr + ab_im * li) / den
    k_im = (ab_im * lr - nr * li) / den
    br = br_ref[...]
    bi = bi_ref[...]
    ar_ref[...] = ab_re
    ai_ref[...] = ab_im
    bbr_ref[...] = k_re * br - k_im * bi
    bbi_ref[...] = k_re * bi + k_im * br


def _discretise(lam_re, lam_im, log_dt, b_re, b_im):
    lead = lam_re.shape[:-1]
    p = lam_re.shape[-1]
    h = b_re.shape[-1]
    r = 1
    for s in lead:
        r *= s
    lr = lam_re.reshape(r, 1, p)
    li = lam_im.reshape(r, 1, p)
    ldt = jnp.broadcast_to(log_dt.reshape(r, 1, 1), (r, 1, p))
    br = jnp.swapaxes(b_re, -1, -2).reshape(r, h, p)
    bi = jnp.swapaxes(b_im, -1, -2).reshape(r, h, p)
    full = lambda shape: pl.BlockSpec(shape, lambda i: (0,) * len(shape))
    ar, ai, bbr, bbi = pl.pallas_call(
        _disc_kernel,
        grid=(1,),
        in_specs=[full((r, 1, p)), full((r, 1, p)), full((r, 1, p)), full((r, h, p)), full((r, h, p))],
        out_specs=[full((r, 1, p)), full((r, 1, p)), full((r, h, p)), full((r, h, p))],
        out_shape=[jax.ShapeDtypeStruct((r, 1, p), F32), jax.ShapeDtypeStruct((r, 1, p), F32),
                   jax.ShapeDtypeStruct((r, h, p), F32), jax.ShapeDtypeStruct((r, h, p), F32)],
        compiler_params=_params("arbitrary"),
        name="s5_discretise",
    )(lr, li, ldt, br, bi)
    return (ar.reshape(lead + (p,)), ai.reshape(lead + (p,)),
            bbr.reshape(lead + (h, p)), bbi.reshape(lead + (h, p)))


def _block_diag(m):
    g = m.shape[-3]
    eye = jnp.eye(g, dtype=m.dtype)
    out = m[..., :, :, None, :] * eye[:, None, :, None]
    return out.reshape(m.shape[:-3] + (g * m.shape[-2], g * m.shape[-1]))


def _s5_operands(ar, ai, bbr, bbi, c_re, c_im):
    lead = ar.shape[:-2]
    g, p = ar.shape[-2:]
    h = bbr.shape[-2]
    nb = g // S5_BLOCK_GROUPS
    bg = S5_BLOCK_GROUPS
    blk = lambda m: m.reshape(lead + (nb, bg) + m.shape[-2:])
    b_in = jnp.concatenate([_block_diag(blk(bbr)), _block_diag(blk(bbi))], axis=-1)
    ct = lambda m: jnp.swapaxes(m, -1, -2)
    c_out = jnp.concatenate([_block_diag(blk(ct(c_re))), -_block_diag(blk(ct(c_im)))], axis=-2)
    a_re = ar.reshape(lead + (nb, 1, bg * p))
    a_im = ai.reshape(lead + (nb, 1, bg * p))
    return b_in.astype(BF16), c_out.astype(BF16), a_re, a_im


def _inproj_kernel(h_ref, mod_ref, g_ref, w_ref, u_ref, a_ref, sg_ref, *, batch, d_s5, d_conv):
    nl = _norm_modulate(h_ref[...], g_ref[...], mod_ref, batch, 0, 1).astype(BF16)
    u_ref[...] = _dot(nl, w_ref[:, :d_s5])
    off = d_s5
    v1 = _dot(nl, w_ref[:, off:off + d_conv])
    v2 = _dot(nl, w_ref[:, off + d_conv:off + 2 * d_conv])
    a_ref[...] = v1 * jax.nn.sigmoid(v2)
    off = d_s5 + 2 * d_conv
    d = h_ref.shape[1]
    for k in range(2):
        sg_ref[:, k * d:(k + 1) * d] = jax.nn.sigmoid(_dot(nl, w_ref[:, off + k * d:off + (k + 1) * d]))


def _inproj(h, mod, gain, w_in, *, batch, n_ctx_tiles, d_s5, d_conv):
    rows, d = h.shape
    tr = MIX_STEPS * batch
    d_in = w_in.shape[1]
    kind = lambda i: (jnp.where(i >= n_ctx_tiles, 1, 0), 0, 0)
    return pl.pallas_call(
        functools.partial(_inproj_kernel, batch=batch, d_s5=d_s5, d_conv=d_conv),
        grid=(rows // tr,),
        in_specs=[pl.BlockSpec((tr, d), lambda i: (i, 0)),
                  pl.BlockSpec((None, batch, N_MOD * d), kind),
                  pl.BlockSpec((1, d), lambda i: (0, 0)),
                  pl.BlockSpec((d, d_in), lambda i: (0, 0))],
        out_specs=[pl.BlockSpec((tr, d_s5), lambda i: (i, 0)),
                   pl.BlockSpec((tr, d_conv), lambda i: (i, 0)),
                   pl.BlockSpec((tr, 2 * d), lambda i: (i, 0))],
        out_shape=[jax.ShapeDtypeStruct((rows, d_s5), F32),
                   jax.ShapeDtypeStruct((rows, d_conv), F32),
                   jax.ShapeDtypeStruct((rows, 2 * d), F32)],
        compiler_params=_params("arbitrary"),
        name="mix_inproj",
    )(h, mod, gain, w_in)


def _s5_kernel(uf_ref, ub_ref, bin_ref, cout_ref, are_ref, aim_ref, yf_ref, yb_ref,
               buf_f, buf_b, hf, hb, *, batch, steps):
    k = pl.program_id(1)
    ns = are_ref.shape[-1]

    @pl.when(k == 0)
    def _():
        hf[...] = jnp.zeros_like(hf)
        hb[...] = jnp.zeros_like(hb)

    buf_f[...] = _dot(uf_ref[...].astype(BF16), bin_ref[0])
    buf_b[...] = _dot(ub_ref[...].astype(BF16), bin_ref[1])

    def scan(buf, carry, direction, reverse):
        a_re = jnp.broadcast_to(are_ref[direction], (batch, ns))
        a_im = jnp.broadcast_to(aim_ref[direction], (batch, ns))

        def step(s, state):
            h_re, h_im = state
            t = (steps - 1 - s) if reverse else s
            r0 = pl.multiple_of(t * batch, batch)
            n_re = a_re * h_re - a_im * h_im + buf[pl.ds(r0, batch), 0:ns]
            n_im = a_re * h_im + a_im * h_re + buf[pl.ds(r0, batch), ns:2 * ns]
            buf[pl.ds(r0, batch), 0:ns] = n_re
            buf[pl.ds(r0, batch), ns:2 * ns] = n_im
            return n_re, n_im

        h_re, h_im = lax.fori_loop(0, steps, step, (carry[:, 0:ns], carry[:, ns:2 * ns]))
        carry[:, 0:ns] = h_re
        carry[:, ns:2 * ns] = h_im

    scan(buf_f, hf, 0, False)
    scan(buf_b, hb, 1, True)
    yf_ref[...] = _dot(buf_f[...].astype(BF16), cout_ref[0])
    yb_ref[...] = _dot(buf_b[...].astype(BF16), cout_ref[1])


def _s5_scan(u, b_in, c_out, a_re, a_im, *, batch, n_ctx_tiles):
    rows, d_s5 = u.shape
    nb = b_in.shape[1]
    ns = a_re.shape[-1]
    tr = MIX_STEPS * batch
    nt = rows // tr

    def rev(k):
        return jnp.where(k < n_ctx_tiles, n_ctx_tiles - 1 - k, nt - 1 + n_ctx_tiles - k)

    return pl.pallas_call(
        functools.partial(_s5_kernel, batch=batch, steps=MIX_STEPS),
        grid=(nb, nt),
        in_specs=[pl.BlockSpec((tr, LANES), lambda j, k: (k, j)),
                  pl.BlockSpec((tr, LANES), lambda j, k: (rev(k), j)),
                  pl.BlockSpec((2, None, LANES, 2 * ns), lambda j, k: (0, j, 0, 0)),
                  pl.BlockSpec((2, None, 2 * ns, LANES), lambda j, k: (0, j, 0, 0)),
                  pl.BlockSpec((2, None, 1, ns), lambda j, k: (0, j, 0, 0)),
                  pl.BlockSpec((2, None, 1, ns), lambda j, k: (0, j, 0, 0))],
        out_specs=[pl.BlockSpec((tr, LANES), lambda j, k: (k, j)),
                   pl.BlockSpec((tr, LANES), lambda j, k: (rev(k), j))],
        out_shape=[jax.ShapeDtypeStruct((rows, d_s5), F32), jax.ShapeDtypeStruct((rows, d_s5), F32)],
        scratch_shapes=[pltpu.VMEM((tr, 2 * ns), F32), pltpu.VMEM((tr, 2 * ns), F32),
                        pltpu.VMEM((batch, 2 * ns), F32), pltpu.VMEM((batch, 2 * ns), F32)],
        compiler_params=_params("arbitrary", "arbitrary"),
        name="s5_scan",
    )(u, u, b_in, c_out, a_re, a_im)


def _mixout_kernel(h_ref, mod_ref, u_ref, yf_ref, yb_ref, ac_ref, ap_ref, an_ref, sg_ref,
                   sd_ref, wglu_ref, wa_ref, dw_ref, dwb_ref, lng_ref, lnb_ref, wb_ref, wout_ref,
                   o_ref, aext, conv, *, batch, n_ctx_tiles, n_tiles):
    i = pl.program_id(0)
    tr, d = h_ref.shape
    halo = ap_ref.shape[0]
    taps = dw_ref.shape[0]

    ys = sd_ref[...] * u_ref[...] + yf_ref[...] + yb_ref[...]
    g = jax.nn.gelu(ys)
    ya = _dot((g * jax.nn.sigmoid(_dot(g.astype(BF16), wglu_ref[...]))).astype(BF16), wa_ref[...])

    first = jnp.logical_or(i == 0, i == n_ctx_tiles)
    last = jnp.logical_or(i == n_ctx_tiles - 1, i == n_tiles - 1)
    aext[0:halo] = jnp.where(first, 0.0, ap_ref[...])
    aext[halo:halo + tr] = ac_ref[...]
    aext[halo + tr:2 * halo + tr] = jnp.where(last, 0.0, an_ref[...])
    chunk = 2 * batch

    def conv_chunk(c, _):
        r0 = pl.multiple_of(c * chunk, chunk)
        acc = jnp.broadcast_to(dwb_ref[...], (chunk, dwb_ref.shape[1]))
        for k in range(taps):
            off = halo + (k - taps // 2) * batch
            acc = acc + dw_ref[k:k + 1, :] * aext[pl.ds(r0 + off, chunk), :]
        conv[pl.ds(r0, chunk), :] = acc
        return 0

    lax.fori_loop(0, tr // chunk, conv_chunk, 0)
    a = conv[...]
    mu = jnp.mean(a, axis=-1, keepdims=True)
    ac = a - mu
    var = jnp.mean(ac * ac, axis=-1, keepdims=True)
    ln = (ac * lax.rsqrt(var + EPS)) * lng_ref[...] + lnb_ref[...]
    yb = _dot((ln * jax.nn.sigmoid(ln)).astype(BF16), wb_ref[...])

    merged = sg_ref[:, 0:d] * ya + sg_ref[:, d:2 * d] * yb
    y = _dot(merged.astype(BF16), wout_ref[...])
    o_ref[...] = _gated_residual(h_ref[...], y, mod_ref, batch, 2)


def _mixout(h, mod, u, yf, yb, a, sg, s5_d, w_glu, w_a, conv_dw, conv_dw_b, ln_g, ln_b, w_b, w_out,
            *, batch, n_ctx_tiles):
    rows, d = h.shape
    d_s5 = u.shape[1]
    d_conv = a.shape[1]
    tr = MIX_STEPS * batch
    halo = CONV_HALO_STEPS * batch
    ratio = tr // halo
    nt = rows // tr
    n_halo_blocks = rows // halo
    taps = conv_dw.shape[0]
    assert taps // 2 <= CONV_HALO_STEPS
    kind = lambda i: (jnp.where(i >= n_ctx_tiles, 1, 0), 0, 0)
    row_tile = lambda w: pl.BlockSpec((tr, w), lambda i: (i, 0))
    const = lambda s: pl.BlockSpec(s, lambda i: (0,) * len(s))
    return pl.pallas_call(
        functools.partial(_mixout_kernel, batch=batch, n_ctx_tiles=n_ctx_tiles, n_tiles=nt),
        grid=(nt,),
        in_specs=[row_tile(d),
                  pl.BlockSpec((None, batch, N_MOD * d), kind),
                  row_tile(d_s5), row_tile(d_s5), row_tile(d_s5),
                  row_tile(d_conv),
                  pl.BlockSpec((halo, d_conv), lambda i: (jnp.maximum(i * ratio - 1, 0), 0)),
                  pl.BlockSpec((halo, d_conv), lambda i: (jnp.minimum((i + 1) * ratio, n_halo_blocks - 1), 0)),
                  row_tile(2 * d),
                  const((1, d_s5)), const((d_s5, d_s5)), const((d_s5, d)),
                  const((taps, d_conv)), const((1, d_conv)), const((1, d_conv)), const((1, d_conv)),
                  const((d_conv, d)), const((d, d))],
        out_specs=row_tile(d),
        out_shape=jax.ShapeDtypeStruct((rows, d), F32),
        scratch_shapes=[pltpu.VMEM((tr + 2 * halo, d_conv), F32), pltpu.VMEM((tr, d_conv), F32)],
        compiler_params=_params("arbitrary"),
        name="mix_out",
    )(h, mod, u, yf, yb, a, a, a, sg, s5_d, w_glu, w_a, conv_dw, conv_dw_b, ln_g, ln_b, w_b, w_out)


def _ffn_gate_kernel(h_ref, mod_ref, g_ref, w_ref, o_ref, *, batch):
    nl = _norm_modulate(h_ref[...], g_ref[...], mod_ref, batch, 3, 4).astype(BF16)
    d_ff = o_ref.shape[1]
    for c in range(d_ff // FFN_BLOCK):
        cols = slice(c * FFN_BLOCK, (c + 1) * FFN_BLOCK)
        o_ref[:, cols] = _dot(nl, w_ref[:, cols]).astype(BF16)


def _ffn_gate(h, mod, gain, w_up, *, batch, n_ctx_tiles, d_ff):
    rows, d = h.shape
    tr = GRID_W * batch
    kind = lambda i: (jnp.where(i >= n_ctx_tiles, 1, 0), 0, 0)
    return pl.pallas_call(
        functools.partial(_ffn_gate_kernel, batch=batch),
        grid=(rows // tr,),
        in_specs=[pl.BlockSpec((tr, d), lambda i: (i, 0)),
                  pl.BlockSpec((None, batch, N_MOD * d), kind),
                  pl.BlockSpec((1, d), lambda i: (0, 0)),
                  pl.BlockSpec((d, d_ff), lambda i: (0, 0))],
        out_specs=pl.BlockSpec((tr, d_ff), lambda i: (i, 0)),
        out_shape=jax.ShapeDtypeStruct((rows, d_ff), BF16),
        compiler_params=_params("arbitrary"),
        name="ffn_gate",
    )(h, mod, gain, w_up)


def _ffn_kernel(h_ref, mod_ref, g_ref, gp_ref, gc_ref, gn_ref, wv_ref, wd_ref, dw_ref, db_ref,
                o_ref, nl_ref, act_ref, *, batch, n_ctx_tiles, n_tiles):
    i = pl.program_id(0)
    f = pl.program_id(1)
    nf = pl.num_programs(1)
    tr, d = h_ref.shape
    fw = gc_ref.shape[1]
    ksz = 3

    @pl.when(f == 0)
    def _():
        nl_ref[...] = _norm_modulate(h_ref[...], g_ref[...], mod_ref, batch, 3, 4).astype(BF16)
        o_ref[...] = jnp.zeros_like(o_ref)

    is_ctx = i < n_ctx_tiles
    up_ok = jnp.logical_and(jnp.logical_not(is_ctx), i >= n_ctx_tiles + 1)
    down_ok = jnp.logical_and(jnp.logical_not(is_ctx), i <= n_tiles - 2)
    ctx_lo = jnp.logical_and(is_ctx, i >= 1)
    ctx_hi = jnp.logical_and(is_ctx, i <= n_ctx_tiles - 2)
    row_ok = (up_ok, None, down_ok)
    srcs = (gp_ref, gc_ref, gn_ref)
    group = FFN_ROW_BLOCK // batch
    n_blocks = tr // FFN_ROW_BLOCK

    def conv_block(rb):
        r0 = rb * FFN_ROW_BLOCK
        for c in range(fw // LANES):
            cols = slice(c * LANES, (c + 1) * LANES)
            w = []
            for s in range(ksz):
                for k in range(ksz):
                    wk = dw_ref[s * ksz + k:s * ksz + k + 1, cols]
                    if row_ok[s] is not None:
                        wk = jnp.where(row_ok[s], wk, 0.0)
                    w.append(jnp.broadcast_to(wk, (batch, LANES)))
            bias = jnp.broadcast_to(db_ref[:, cols], (batch, LANES))
            def step_rows(s, rj):
                if rj < 0:
                    return jnp.where(ctx_lo, gp_ref[tr - batch:tr, cols].astype(F32), 0.0) if s == 1 else None
                if rj >= tr:
                    return jnp.where(ctx_hi, gn_ref[0:batch, cols].astype(F32), 0.0) if s == 1 else None
                return srcs[s][rj:rj + batch, cols].astype(F32)

            window = [[step_rows(s, r0 + j * batch) for j in (-1, 0)] for s in range(ksz)]
            for j in range(group):
                rj = r0 + j * batch
                acc = bias
                for s in range(ksz):
                    window[s].append(step_rows(s, rj + batch))
                    for k in range(ksz):
                        if window[s][k] is not None:
                            acc = acc + w[s * ksz + k] * window[s][k]
                    window[s].pop(0)
                act_ref[rj:rj + batch, cols] = (acc * jax.nn.sigmoid(acc)).astype(BF16)

    def project_block(rb):
        rows = slice(rb * FFN_ROW_BLOCK, (rb + 1) * FFN_ROW_BLOCK)
        nl = nl_ref[rows, :]
        for c0 in range(0, fw, MXU_N):
            cols = slice(c0, min(c0 + MXU_N, fw))
            act_ref[rows, cols] = (act_ref[rows, cols].astype(F32) * _dot(nl, wv_ref[:, cols])).astype(BF16)
        o_ref[rows, :] += _dot(act_ref[rows, :], wd_ref[...])

    conv_block(0)
    for rb in range(n_blocks - 1):
        conv_block(rb + 1)
        project_block(rb)
    project_block(n_blocks - 1)

    @pl.when(f == nf - 1)
    def _():
        o_ref[...] = _gated_residual(h_ref[...], o_ref[...], mod_ref, batch, 5)


def _ffn_block(d_ff):
    half = d_ff // 2
    return half if half % LANES == 0 else d_ff


def _ffn(h, mod, gain, gate, w_up, w_down, dw, dw_b, *, batch, n_ctx_tiles, d_ff):
    rows, d = h.shape
    tr = GRID_W * batch
    nt = rows // tr
    fw = _ffn_block(d_ff)
    nf = d_ff // fw
    kind = lambda i, f: (jnp.where(i >= n_ctx_tiles, 1, 0), 0, 0)
    return pl.pallas_call(
        functools.partial(_ffn_kernel, batch=batch, n_ctx_tiles=n_ctx_tiles, n_tiles=nt),
        grid=(nt, nf),
        in_specs=[pl.BlockSpec((tr, d), lambda i, f: (i, 0)),
                  pl.BlockSpec((None, batch, N_MOD * d), kind),
                  pl.BlockSpec((1, d), lambda i, f: (0, 0)),
                  pl.BlockSpec((tr, fw), lambda i, f: (jnp.maximum(i - 1, 0), f)),
                  pl.BlockSpec((tr, fw), lambda i, f: (i, f)),
                  pl.BlockSpec((tr, fw), lambda i, f: (jnp.minimum(i + 1, nt - 1), f)),
                  pl.BlockSpec((d, fw), lambda i, f: (0, nf + f)),
                  pl.BlockSpec((fw, d), lambda i, f: (f, 0)),
                  pl.BlockSpec((9, fw), lambda i, f: (0, f)),
                  pl.BlockSpec((1, fw), lambda i, f: (0, f))],
        out_specs=pl.BlockSpec((tr, d), lambda i, f: (i, 0)),
        out_shape=jax.ShapeDtypeStruct((rows, d), F32),
        scratch_shapes=[pltpu.VMEM((tr, d), BF16), pltpu.VMEM((tr, fw), BF16)],
        compiler_params=pltpu.CompilerParams(dimension_semantics=("arbitrary", "arbitrary"),
                                             vmem_limit_bytes=FFN_VMEM_LIMIT),
        name="ffn_main",
    )(h, mod, gain, gate, gate, gate, w_up, w_down, dw, dw_b)


def _final_norm_kernel(h_ref, g_ref, o_ref):
    x = h_ref[...]
    ms = jnp.mean(x * x, axis=-1, keepdims=True)
    o_ref[...] = (x * lax.rsqrt(ms + EPS)) * g_ref[...]


def _final_norm(h, gain, *, first_row, out_rows):
    d = h.shape[1]
    tr = 1024
    off = first_row // tr
    return pl.pallas_call(
        _final_norm_kernel,
        grid=(out_rows // tr,),
        in_specs=[pl.BlockSpec((tr, d), lambda i: (i + off, 0)),
                  pl.BlockSpec((1, d), lambda i: (0, 0))],
        out_specs=pl.BlockSpec((tr, d), lambda i: (i, 0)),
        out_shape=jax.ShapeDtypeStruct((out_rows, d), F32),
        compiler_params=_params("arbitrary"),
        name="final_norm",
    )(h, gain)


def kernel(x, c, ctx, c_ctx, ada_w, ada_b, norm1_g, w_in, s5_lam_re, s5_lam_im, s5_log_dt, s5_b_re, s5_b_im, s5_c_re, s5_c_im, s5_d, w_glu, w_a, conv_dw, conv_dw_b, conv_ln_g, conv_ln_b, w_b, w_out, norm2_g, ffn_w_up, ffn_dw, ffn_dw_b, ffn_w_down, final_g):
    batch, seq, d = x.shape
    ctx_len = ctx.shape[1]
    depth = ada_w.shape[0]
    d_s5 = w_glu.shape[1]
    d_conv = conv_dw.shape[2]
    d_ff = ffn_dw.shape[-1]
    assert batch % 16 == 0 and seq % GRID_W == 0 and ctx_len % GRID_W == 0
    assert d_s5 % LANES == 0 and d_ff % FFN_BLOCK == 0
    mix_ctx_tiles = ctx_len // MIX_STEPS
    ffn_ctx_tiles = ctx_len // GRID_W

    h = jnp.concatenate([jnp.swapaxes(ctx, 0, 1), jnp.swapaxes(x, 0, 1)], axis=0).reshape(-1, d)

    cc = jnp.concatenate([jnp.broadcast_to(c_ctx[None], (batch, d)), c], axis=0)
    mods = _modulation(cc, ada_w, ada_b).reshape(depth, 2, batch, N_MOD * d)

    ar, ai, bbr, bbi = _discretise(s5_lam_re, s5_lam_im, s5_log_dt, s5_b_re, s5_b_im)
    b_in, c_out, a_re, a_im = _s5_operands(ar, ai, bbr, bbi, s5_c_re, s5_c_im)

    row = lambda v: v.reshape(1, -1)
    for i in range(depth):
        mod = mods[i]
        u, a, sg = _inproj(h, mod, row(norm1_g[i]), w_in[i].astype(BF16), batch=batch,
                           n_ctx_tiles=mix_ctx_tiles, d_s5=d_s5, d_conv=d_conv)
        yf, yb = _s5_scan(u, b_in[i], c_out[i], a_re[i], a_im[i], batch=batch, n_ctx_tiles=mix_ctx_tiles)
        h = _mixout(h, mod, u, yf, yb, a, sg, row(s5_d[i]), w_glu[i].astype(BF16), w_a[i].astype(BF16),
                    conv_dw[i], row(conv_dw_b[i]), row(conv_ln_g[i]), row(conv_ln_b[i]),
                    w_b[i].astype(BF16), w_out[i].astype(BF16), batch=batch, n_ctx_tiles=mix_ctx_tiles)
        w_up = ffn_w_up[i].astype(BF16)
        gate = _ffn_gate(h, mod, row(norm2_g[i]), w_up, batch=batch, n_ctx_tiles=ffn_ctx_tiles, d_ff=d_ff)
        h = _ffn(h, mod, row(norm2_g[i]), gate, w_up, ffn_w_down[i].astype(BF16),
                 ffn_dw[i].reshape(9, d_ff), row(ffn_dw_b[i]), batch=batch, n_ctx_tiles=ffn_ctx_tiles, d_ff=d_ff)

    out = _final_norm(h, row(final_g), first_row=ctx_len * batch, out_rows=seq * batch)
    return jnp.swapaxes(out.reshape(seq, batch, d), 0, 1)
```

```python
import functools

import jax
import jax.numpy as jnp
from jax import lax
from jax.experimental import pallas as pl
from jax.experimental.pallas import tpu as pltpu

GRID_W = 64
S5_GROUP = 16
S5_STATE = 64
N_MOD = 6
EPS = 1e-6
LANES = 128
S5_BLOCK_GROUPS = LANES // S5_GROUP
MIX_STEPS = 32
CONV_HALO_STEPS = 16
FFN_BLOCK = 256
FFN_ROW_BLOCK = 256
MXU_N = 256
VMEM_LIMIT = 56 * 1024 * 1024
FFN_VMEM_LIMIT = 60 * 1024 * 1024

BF16 = jnp.bfloat16
F32 = jnp.float32


def _dot(a, b):
    return jnp.dot(a, b, preferred_element_type=F32)


def _split_bf16(a):
    hi = a.astype(BF16)
    lo = (a - hi.astype(F32)).astype(BF16)
    return hi, lo


def _dot_3pass(a, b):
    a_hi, a_lo = _split_bf16(a)
    b_hi, b_lo = _split_bf16(b)
    return _dot(a_hi, b_hi) + (_dot(a_lo, b_hi) + _dot(a_hi, b_lo))


def _params(*sem):
    return pltpu.CompilerParams(dimension_semantics=sem, vmem_limit_bytes=VMEM_LIMIT)


def _norm_modulate(x, gain, mod_ref, batch, k_shift, k_scale):
    rows, d = x.shape
    ms = jnp.mean(x * x, axis=-1, keepdims=True)
    n = (x * lax.rsqrt(ms + EPS)) * gain
    shift = mod_ref[:, k_shift * d:(k_shift + 1) * d]
    scale = mod_ref[:, k_scale * d:(k_scale + 1) * d]
    n3 = n.reshape(rows // batch, batch, d)
    out = n3 * (1.0 + scale)[None] + shift[None]
    return out.reshape(rows, d)


def _gated_residual(x, y, mod_ref, batch, k_gate):
    rows, d = x.shape
    gate = mod_ref[:, k_gate * d:(k_gate + 1) * d]
    out = x.reshape(rows // batch, batch, d) + gate[None] * y.reshape(rows // batch, batch, d)
    return out.reshape(rows, d)


def _mod_kernel(c_ref, w_ref, b_ref, o_ref):
    c = c_ref[...]
    s = c * jax.nn.sigmoid(c)
    o_ref[...] = _dot_3pass(s, w_ref[...]) + b_ref[...]


def _modulation(cc, ada_w, ada_b):
    depth, d, n = ada_w.shape
    rows = cc.shape[0]
    tn = d
    return pl.pallas_call(
        _mod_kernel,
        grid=(depth, n // tn),
        in_specs=[pl.BlockSpec((rows, d), lambda l, j: (0, 0)),
                  pl.BlockSpec((None, d, tn), lambda l, j: (l, 0, j)),
                  pl.BlockSpec((None, 1, tn), lambda l, j: (l, 0, j))],
        out_specs=pl.BlockSpec((None, rows, tn), lambda l, j: (l, 0, j)),
        out_shape=jax.ShapeDtypeStruct((depth, rows, n), F32),
        compiler_params=_params("arbitrary", "arbitrary"),
        name="adaln_mod",
    )(cc, ada_w, ada_b.reshape(depth, 1, n))


def _disc_kernel(lr_ref, li_ref, ldt_ref, br_ref, bi_ref, ar_ref, ai_ref, bbr_ref, bbi_ref):
    dt = jnp.exp(ldt_ref[...])
    lr = lr_ref[...]
    li = li_ref[...]
    mag = jnp.exp(dt * lr)
    ab_re = mag * jnp.cos(dt * li)
    ab_im = mag * jnp.sin(dt * li)
    den = lr * lr + li * li
    nr = ab_re - 1.0
    k_re = (nr * lr + ab_im * li) / den
    k_im = (ab_im * lr - nr * li) / den
    br = br_ref[...]
    bi = bi_ref[...]
    ar_ref[...] = ab_re
    ai_ref[...] = ab_im
    bbr_ref[...] = k_re * br - k_im * bi
    bbi_ref[...] = k_re * bi + k_im * br


def _discretise(lam_re, lam_im, log_dt, b_re, b_im):
    lead = lam_re.shape[:-1]
    p = lam_re.shape[-1]
    h = b_re.shape[-1]
    r = 1
    for s in lead:
        r *= s
    lr = lam_re.reshape(r, 1, p)
    li = lam_im.reshape(r, 1, p)
    ldt = jnp.broadcast_to(log_dt.reshape(r, 1, 1), (r, 1, p))
    br = jnp.swapaxes(b_re, -1, -2).reshape(r, h, p)
    bi = jnp.swapaxes(b_im, -1, -2).reshape(r, h, p)
    full = lambda shape: pl.BlockSpec(shape, lambda i: (0,) * len(shape))
    ar, ai, bbr, bbi = pl.pallas_call(
        _disc_kernel,
        grid=(1,),
        in_specs=[full((r, 1, p)), full((r, 1, p)), full((r, 1, p)), full((r, h, p)), full((r, h, p))],
        out_specs=[full((r, 1, p)), full((r, 1, p)), full((r, h, p)), full((r, h, p))],
        out_shape=[jax.ShapeDtypeStruct((r, 1, p), F32), jax.ShapeDtypeStruct((r, 1, p), F32),
                   jax.ShapeDtypeStruct((r, h, p), F32), jax.ShapeDtypeStruct((r, h, p), F32)],
        compiler_params=_params("arbitrary"),
        name="s5_discretise",
    )(lr, li, ldt, br, bi)
    return (ar.reshape(lead + (p,)), ai.reshape(lead + (p,)),
            bbr.reshape(lead + (h, p)), bbi.reshape(lead + (h, p)))


def _block_diag(m):
    g = m.shape[-3]
    eye = jnp.eye(g, dtype=m.dtype)
    out = m[..., :, :, None, :] * eye[:, None, :, None]
    return out.reshape(m.shape[:-3] + (g * m.shape[-2], g * m.shape[-1]))


def _s5_operands(ar, ai, bbr, bbi, c_re, c_im):
    lead = ar.shape[:-2]
    g, p = ar.shape[-2:]
    h = bbr.shape[-2]
    nb = g // S5_BLOCK_GROUPS
    bg = S5_BLOCK_GROUPS
    blk = lambda m: m.reshape(lead + (nb, bg) + m.shape[-2:])
    b_in = jnp.concatenate([_block_diag(blk(bbr)), _block_diag(blk(bbi))], axis=-1)
    ct = lambda m: jnp.swapaxes(m, -1, -2)
    c_out = jnp.concatenate([_block_diag(blk(ct(c_re))), -_block_diag(blk(ct(c_im)))], axis=-2)
    a_re = ar.reshape(lead + (nb, 1, bg * p))
    a_im = ai.reshape(lead + (nb, 1, bg * p))
    return b_in.astype(BF16), c_out.astype(BF16), a_re, a_im


def _inproj_kernel(h_ref, mod_ref, g_ref, w_ref, u_ref, a_ref, sg_ref, *, batch, d_s5, d_conv):
    nl = _norm_modulate(h_ref[...], g_ref[...], mod_ref, batch, 0, 1).astype(BF16)
    u_ref[...] = _dot(nl, w_ref[:, :d_s5])
    off = d_s5
    v1 = _dot(nl, w_ref[:, off:off + d_conv])
    v2 = _dot(nl, w_ref[:, off + d_conv:off + 2 * d_conv])
    a_ref[...] = v1 * jax.nn.sigmoid(v2)
    off = d_s5 + 2 * d_conv
    d = h_ref.shape[1]
    for k in range(2):
        sg_ref[:, k * d:(k + 1) * d] = jax.nn.sigmoid(_dot(nl, w_ref[:, off + k * d:off + (k + 1) * d]))


def _inproj(h, mod, gain, w_in, *, batch, n_ctx_tiles, d_s5, d_conv):
    rows, d = h.shape
    tr = MIX_STEPS * batch
    d_in = w_in.shape[1]
    kind = lambda i: (jnp.where(i >= n_ctx_tiles, 1, 0), 0, 0)
    return pl.pallas_call(
        functools.partial(_inproj_kernel, batch=batch, d_s5=d_s5, d_conv=d_conv),
        grid=(rows // tr,),
        in_specs=[pl.BlockSpec((tr, d), lambda i: (i, 0)),
                  pl.BlockSpec((None, batch, N_MOD * d), kind),
                  pl.BlockSpec((1, d), lambda i: (0, 0)),
                  pl.BlockSpec((d, d_in), lambda i: (0, 0))],
        out_specs=[pl.BlockSpec((tr, d_s5), lambda i: (i, 0)),
                   pl.BlockSpec((tr, d_conv), lambda i: (i, 0)),
                   pl.BlockSpec((tr, 2 * d), lambda i: (i, 0))],
        out_shape=[jax.ShapeDtypeStruct((rows, d_s5), F32),
                   jax.ShapeDtypeStruct((rows, d_conv), F32),
                   jax.ShapeDtypeStruct((rows, 2 * d), F32)],
        compiler_params=_params("arbitrary"),
        name="mix_inproj",
    )(h, mod, gain, w_in)


def _s5_kernel(uf0_ref, uf1_ref, uf2_ref, ub0_ref, ub1_ref, ub2_ref, bin_ref, cout_ref, are_ref, aim_ref,
               yf_ref, yb_ref, bufs, carry, *, batch, steps):
    m = pl.program_id(1)
    ns = are_ref.shape[-1]
    tr = steps * batch
    u_first = (uf0_ref, ub0_ref)
    u_odd = (uf1_ref, ub1_ref)
    u_next = (uf2_ref, ub2_ref)
    y_refs = (yf_ref, yb_ref)

    def project_in(u_ref, direction, slot):
        bufs[direction, slot] = _dot(u_ref[...].astype(BF16), bin_ref[direction])

    def recur(direction, slot):
        a_re = jnp.broadcast_to(are_ref[direction], (batch, ns))
        a_im = jnp.broadcast_to(aim_ref[direction], (batch, ns))
        h_re = carry[direction, :, 0:ns]
        h_im = carry[direction, :, ns:2 * ns]
        for s in range(steps):
            t = (steps - 1 - s) if direction == 1 else s
            rows = slice(t * batch, (t + 1) * batch)
            n_re = a_re * h_re - a_im * h_im + bufs[direction, slot, rows, 0:ns]
            n_im = a_re * h_im + a_im * h_re + bufs[direction, slot, rows, ns:2 * ns]
            bufs[direction, slot, rows, 0:ns] = n_re
            bufs[direction, slot, rows, ns:2 * ns] = n_im
            h_re, h_im = n_re, n_im
        carry[direction, :, 0:ns] = h_re
        carry[direction, :, ns:2 * ns] = h_im

    def project_out(direction, slot, position):
        half = position if direction == 0 else 1 - position
        y_refs[direction][half * tr:(half + 1) * tr, :] = _dot(bufs[direction, slot].astype(BF16), cout_ref[direction])

    @pl.when(m == 0)
    def _():
        carry[...] = jnp.zeros_like(carry)
        for direction in range(2):
            project_in(u_first[direction], direction, 0)

    for direction in range(2):
        project_in(u_odd[direction], direction, 1)
        recur(direction, 0)
    for direction in range(2):
        project_out(direction, 0, 0)
        recur(direction, 1)
    for direction in range(2):
        project_in(u_next[direction], direction, 0)
        project_out(direction, 1, 1)


def _s5_scan(u, b_in, c_out, a_re, a_im, *, batch, n_ctx_tiles):
    rows, d_s5 = u.shape
    nb = b_in.shape[1]
    ns = a_re.shape[-1]
    tr = MIX_STEPS * batch
    nt = rows // tr
    assert nt % 2 == 0 and n_ctx_tiles % 2 == 0

    def rev(p):
        return jnp.where(p < n_ctx_tiles, n_ctx_tiles - 1 - p, nt - 1 + n_ctx_tiles - p)

    nxt = lambda m: jnp.minimum(2 * m + 2, nt - 1)
    chunk = lambda index: pl.BlockSpec((tr, LANES), index)
    param = lambda shape: pl.BlockSpec((2, None) + shape, lambda j, m: (0, j, 0, 0))
    return pl.pallas_call(
        functools.partial(_s5_kernel, batch=batch, steps=MIX_STEPS),
        grid=(nb, nt // 2),
        in_specs=[chunk(lambda j, m: (0, j)),
                  chunk(lambda j, m: (2 * m + 1, j)),
                  chunk(lambda j, m: (nxt(m), j)),
                  chunk(lambda j, m: (rev(0), j)),
                  chunk(lambda j, m: (rev(2 * m + 1), j)),
                  chunk(lambda j, m: (rev(nxt(m)), j)),
                  param((LANES, 2 * ns)), param((2 * ns, LANES)), param((1, ns)), param((1, ns))],
        out_specs=[pl.BlockSpec((2 * tr, LANES), lambda j, m: (m, j)),
                   pl.BlockSpec((2 * tr, LANES), lambda j, m: (rev(2 * m + 1) // 2, j))],
        out_shape=[jax.ShapeDtypeStruct((rows, d_s5), F32), jax.ShapeDtypeStruct((rows, d_s5), F32)],
        scratch_shapes=[pltpu.VMEM((2, 2, tr, 2 * ns), F32), pltpu.VMEM((2, batch, 2 * ns), F32)],
        compiler_params=_params("arbitrary", "arbitrary"),
        name="s5_scan",
    )(u, u, u, u, u, u, b_in, c_out, a_re, a_im)


def _mixout_kernel(h_ref, mod_ref, u_ref, yf_ref, yb_ref, ac_ref, ap_ref, an_ref, sg_ref,
                   sd_ref, wglu_ref, wa_ref, dw_ref, dwb_ref, lng_ref, lnb_ref, wb_ref, wout_ref,
                   o_ref, aext, conv, *, batch, n_ctx_tiles, n_tiles):
    i = pl.program_id(0)
    tr, d = h_ref.shape
    halo = ap_ref.shape[0]
    taps = dw_ref.shape[0]

    ys = sd_ref[...] * u_ref[...] + yf_ref[...] + yb_ref[...]
    g = jax.nn.gelu(ys)
    ya = _dot((g * jax.nn.sigmoid(_dot(g.astype(BF16), wglu_ref[...]))).astype(BF16), wa_ref[...])

    first = jnp.logical_or(i == 0, i == n_ctx_tiles)
    last = jnp.logical_or(i == n_ctx_tiles - 1, i == n_tiles - 1)
    aext[0:halo] = jnp.where(first, 0.0, ap_ref[...])
    aext[halo:halo + tr] = ac_ref[...]
    aext[halo + tr:2 * halo + tr] = jnp.where(last, 0.0, an_ref[...])
    chunk = 2 * batch

    def conv_chunk(c, _):
        r0 = pl.multiple_of(c * chunk, chunk)
        acc = jnp.broadcast_to(dwb_ref[...], (chunk, dwb_ref.shape[1]))
        for k in range(taps):
            off = halo + (k - taps // 2) * batch
            acc = acc + dw_ref[k:k + 1, :] * aext[pl.ds(r0 + off, chunk), :]
        conv[pl.ds(r0, chunk), :] = acc
        return 0

    lax.fori_loop(0, tr // chunk, conv_chunk, 0)
    a = conv[...]
    mu = jnp.mean(a, axis=-1, keepdims=True)
    ac = a - mu
    var = jnp.mean(ac * ac, axis=-1, keepdims=True)
    ln = (ac * lax.rsqrt(var + EPS)) * lng_ref[...] + lnb_ref[...]
    yb = _dot((ln * jax.nn.sigmoid(ln)).astype(BF16), wb_ref[...])

    merged = sg_ref[:, 0:d] * ya + sg_ref[:, d:2 * d] * yb
    y = _dot(merged.astype(BF16), wout_ref[...])
    o_ref[...] = _gated_residual(h_ref[...], y, mod_ref, batch, 2)


def _mixout(h, mod, u, yf, yb, a, sg, s5_d, w_glu, w_a, conv_dw, conv_dw_b, ln_g, ln_b, w_b, w_out,
            *, batch, n_ctx_tiles):
    rows, d = h.shape
    d_s5 = u.shape[1]
    d_conv = a.shape[1]
    tr = MIX_STEPS * batch
    halo = CONV_HALO_STEPS * batch
    ratio = tr // halo
    nt = rows // tr
    n_halo_blocks = rows // halo
    taps = conv_dw.shape[0]
    assert taps // 2 <= CONV_HALO_STEPS
    kind = lambda i: (jnp.where(i >= n_ctx_tiles, 1, 0), 0, 0)
    row_tile = lambda w: pl.BlockSpec((tr, w), lambda i: (i, 0))
    const = lambda s: pl.BlockSpec(s, lambda i: (0,) * len(s))
    return pl.pallas_call(
        functools.partial(_mixout_kernel, batch=batch, n_ctx_tiles=n_ctx_tiles, n_tiles=nt),
        grid=(nt,),
        in_specs=[row_tile(d),
                  pl.BlockSpec((None, batch, N_MOD * d), kind),
                  row_tile(d_s5), row_tile(d_s5), row_tile(d_s5),
                  row_tile(d_conv),
                  pl.BlockSpec((halo, d_conv), lambda i: (jnp.maximum(i * ratio - 1, 0), 0)),
                  pl.BlockSpec((halo, d_conv), lambda i: (jnp.minimum((i + 1) * ratio, n_halo_blocks - 1), 0)),
                  row_tile(2 * d),
                  const((1, d_s5)), const((d_s5, d_s5)), const((d_s5, d)),
                  const((taps, d_conv)), const((1, d_conv)), const((1, d_conv)), const((1, d_conv)),
                  const((d_conv, d)), const((d, d))],
        out_specs=row_tile(d),
        out_shape=jax.ShapeDtypeStruct((rows, d), F32),
        scratch_shapes=[pltpu.VMEM((tr + 2 * halo, d_conv), F32), pltpu.VMEM((tr, d_conv), F32)],
        compiler_params=_params("arbitrary"),
        name="mix_out",
    )(h, mod, u, yf, yb, a, a, a, sg, s5_d, w_glu, w_a, conv_dw, conv_dw_b, ln_g, ln_b, w_b, w_out)


def _ffn_gate_kernel(h_ref, mod_ref, g_ref, w_ref, o_ref, *, batch):
    nl = _norm_modulate(h_ref[...], g_ref[...], mod_ref, batch, 3, 4).astype(BF16)
    d_ff = o_ref.shape[1]
    for c in range(d_ff // FFN_BLOCK):
        cols = slice(c * FFN_BLOCK, (c + 1) * FFN_BLOCK)
        o_ref[:, cols] = _dot(nl, w_ref[:, cols]).astype(BF16)


def _ffn_gate(h, mod, gain, w_up, *, batch, n_ctx_tiles, d_ff):
    rows, d = h.shape
    tr = GRID_W * batch
    kind = lambda i: (jnp.where(i >= n_ctx_tiles, 1, 0), 0, 0)
    return pl.pallas_call(
        functools.partial(_ffn_gate_kernel, batch=batch),
        grid=(rows // tr,),
        in_specs=[pl.BlockSpec((tr, d), lambda i: (i, 0)),
                  pl.BlockSpec((None, batch, N_MOD * d), kind),
                  pl.BlockSpec((1, d), lambda i: (0, 0)),
                  pl.BlockSpec((d, d_ff), lambda i: (0, 0))],
        out_specs=pl.BlockSpec((tr, d_ff), lambda i: (i, 0)),
        out_shape=jax.ShapeDtypeStruct((rows, d_ff), BF16),
        compiler_params=_params("arbitrary"),
        name="ffn_gate",
    )(h, mod, gain, w_up)


def _ffn_kernel(h_ref, mod_ref, g_ref, gp_ref, gc_ref, gn_ref, wv_ref, wd_ref, dw_ref, db_ref,
                o_ref, nl_ref, act_ref, *, batch, n_ctx_tiles, n_tiles):
    i = pl.program_id(0)
    f = pl.program_id(1)
    nf = pl.num_programs(1)
    tr, d = h_ref.shape
    fw = gc_ref.shape[1]
    ksz = 3

    @pl.when(f == 0)
    def _():
        nl_ref[...] = _norm_modulate(h_ref[...], g_ref[...], mod_ref, batch, 3, 4).astype(BF16)
        o_ref[...] = jnp.zeros_like(o_ref)

    is_ctx = i < n_ctx_tiles
    up_ok = jnp.logical_and(jnp.logical_not(is_ctx), i >= n_ctx_tiles + 1)
    down_ok = jnp.logical_and(jnp.logical_not(is_ctx), i <= n_tiles - 2)
    ctx_lo = jnp.logical_and(is_ctx, i >= 1)
    ctx_hi = jnp.logical_and(is_ctx, i <= n_ctx_tiles - 2)
    row_ok = (up_ok, None, down_ok)
    srcs = (gp_ref, gc_ref, gn_ref)
    group = FFN_ROW_BLOCK // batch
    n_blocks = tr // FFN_ROW_BLOCK

    def conv_block(rb):
        r0 = rb * FFN_ROW_BLOCK
        for c in range(fw // LANES):
            cols = slice(c * LANES, (c + 1) * LANES)
            w = []
            for s in range(ksz):
                for k in range(ksz):
                    wk = dw_ref[s * ksz + k:s * ksz + k + 1, cols]
                    if row_ok[s] is not None:
                        wk = jnp.where(row_ok[s], wk, 0.0)
                    w.append(jnp.broadcast_to(wk.astype(BF16), (batch, LANES)))
            bias = jnp.broadcast_to(db_ref[:, cols].astype(BF16), (batch, LANES))

            def step_rows(s, rj):
                zero = jnp.zeros((batch, LANES), BF16)
                if rj < 0:
                    return jnp.where(ctx_lo, gp_ref[tr - batch:tr, cols], zero) if s == 1 else None
                if rj >= tr:
                    return jnp.where(ctx_hi, gn_ref[0:batch, cols], zero) if s == 1 else None
                return srcs[s][rj:rj + batch, cols]

            window = [[step_rows(s, r0 + j * batch) for j in (-1, 0)] for s in range(ksz)]
            for j in range(group):
                rj = r0 + j * batch
                acc = bias
                for s in range(ksz):
                    window[s].append(step_rows(s, rj + batch))
                    for k in range(ksz):
                        if window[s][k] is not None:
                            acc = acc + w[s * ksz + k] * window[s][k]
                    window[s].pop(0)
                act_ref[rj:rj + batch, cols] = acc * jax.nn.sigmoid(acc)

    def project_block(rb):
        rows = slice(rb * FFN_ROW_BLOCK, (rb + 1) * FFN_ROW_BLOCK)
        nl = nl_ref[rows, :]
        for c0 in range(0, fw, MXU_N):
            cols = slice(c0, min(c0 + MXU_N, fw))
            act_ref[rows, cols] = (act_ref[rows, cols].astype(F32) * _dot(nl, wv_ref[:, cols])).astype(BF16)
        o_ref[rows, :] += _dot(act_ref[rows, :], wd_ref[...])

    conv_block(0)
    for rb in range(n_blocks - 1):
        conv_block(rb + 1)
        project_block(rb)
    project_block(n_blocks - 1)

    @pl.when(f == nf - 1)
    def _():
        o_ref[...] = _gated_residual(h_ref[...], o_ref[...], mod_ref, batch, 5)


def _ffn_block(d_ff):
    half = d_ff // 2
    return half if half % LANES == 0 else d_ff


def _ffn(h, mod, gain, gate, w_up, w_down, dw, dw_b, *, batch, n_ctx_tiles, d_ff):
    rows, d = h.shape
    tr = GRID_W * batch
    nt = rows // tr
    fw = _ffn_block(d_ff)
    nf = d_ff // fw
    kind = lambda i, f: (jnp.where(i >= n_ctx_tiles, 1, 0), 0, 0)
    return pl.pallas_call(
        functools.partial(_ffn_kernel, batch=batch, n_ctx_tiles=n_ctx_tiles, n_tiles=nt),
        grid=(nt, nf),
        in_specs=[pl.BlockSpec((tr, d), lambda i, f: (i, 0)),
                  pl.BlockSpec((None, batch, N_MOD * d), kind),
                  pl.BlockSpec((1, d), lambda i, f: (0, 0)),
                  pl.BlockSpec((tr, fw), lambda i, f: (jnp.maximum(i - 1, 0), f)),
                  pl.BlockSpec((tr, fw), lambda i, f: (i, f)),
                  pl.BlockSpec((tr, fw), lambda i, f: (jnp.minimum(i + 1, nt - 1), f)),
                  pl.BlockSpec((d, fw), lambda i, f: (0, nf + f)),
                  pl.BlockSpec((fw, d), lambda i, f: (f, 0)),
                  pl.BlockSpec((9, fw), lambda i, f: (0, f)),
                  pl.BlockSpec((1, fw), lambda i, f: (0, f))],
        out_specs=pl.BlockSpec((tr, d), lambda i, f: (i, 0)),
        out_shape=jax.ShapeDtypeStruct((rows, d), F32),
        scratch_shapes=[pltpu.VMEM((tr, d), BF16), pltpu.VMEM((tr, fw), BF16)],
        compiler_params=pltpu.CompilerParams(dimension_semantics=("arbitrary", "arbitrary"),
                                             vmem_limit_bytes=FFN_VMEM_LIMIT),
        name="ffn_main",
    )(h, mod, gain, gate, gate, gate, w_up, w_down, dw, dw_b)


def _final_norm_kernel(h_ref, g_ref, o_ref):
    x = h_ref[...]
    ms = jnp.mean(x * x, axis=-1, keepdims=True)
    o_ref[...] = (x * lax.rsqrt(ms + EPS)) * g_ref[...]


def _final_norm(h, gain, *, first_row, out_rows):
    d = h.shape[1]
    tr = 1024
    off = first_row // tr
    return pl.pallas_call(
        _final_norm_kernel,
        grid=(out_rows // tr,),
        in_specs=[pl.BlockSpec((tr, d), lambda i: (i + off, 0)),
                  pl.BlockSpec((1, d), lambda i: (0, 0))],
        out_specs=pl.BlockSpec((tr, d), lambda i: (i, 0)),
        out_shape=jax.ShapeDtypeStruct((out_rows, d), F32),
        compiler_params=_params("arbitrary"),
        name="final_norm",
    )(h, gain)


def kernel(x, c, ctx, c_ctx, ada_w, ada_b, norm1_g, w_in, s5_lam_re, s5_lam_im, s5_log_dt, s5_b_re, s5_b_im, s5_c_re, s5_c_im, s5_d, w_glu, w_a, conv_dw, conv_dw_b, conv_ln_g, conv_ln_b, w_b, w_out, norm2_g, ffn_w_up, ffn_dw, ffn_dw_b, ffn_w_down, final_g):
    batch, seq, d = x.shape
    ctx_len = ctx.shape[1]
    depth = ada_w.shape[0]
    d_s5 = w_glu.shape[1]
    d_conv = conv_dw.shape[2]
    d_ff = ffn_dw.shape[-1]
    assert batch % 16 == 0 and seq % GRID_W == 0 and ctx_len % GRID_W == 0
    assert d_s5 % LANES == 0 and d_ff % FFN_BLOCK == 0
    mix_ctx_tiles = ctx_len // MIX_STEPS
    ffn_ctx_tiles = ctx_len // GRID_W

    h = jnp.concatenate([jnp.swapaxes(ctx, 0, 1), jnp.swapaxes(x, 0, 1)], axis=0).reshape(-1, d)

    cc = jnp.concatenate([jnp.broadcast_to(c_ctx[None], (batch, d)), c], axis=0)
    mods = _modulation(cc, ada_w, ada_b).reshape(depth, 2, batch, N_MOD * d)

    ar, ai, bbr, bbi = _discretise(s5_lam_re, s5_lam_im, s5_log_dt, s5_b_re, s5_b_im)
    b_in, c_out, a_re, a_im = _s5_operands(ar, ai, bbr, bbi, s5_c_re, s5_c_im)

    row = lambda v: v.reshape(1, -1)
    for i in range(depth):
        mod = mods[i]
        u, a, sg = _inproj(h, mod, row(norm1_g[i]), w_in[i].astype(BF16), batch=batch,
                           n_ctx_tiles=mix_ctx_tiles, d_s5=d_s5, d_conv=d_conv)
        yf, yb = _s5_scan(u, b_in[i], c_out[i], a_re[i], a_im[i], batch=batch, n_ctx_tiles=mix_ctx_tiles)
        h = _mixout(h, mod, u, yf, yb, a, sg, row(s5_d[i]), w_glu[i].astype(BF16), w_a[i].astype(BF16),
                    conv_dw[i], row(conv_dw_b[i]), row(conv_ln_g[i]), row(conv_ln_b[i]),
                    w_b[i].astype(BF16), w_out[i].astype(BF16), batch=batch, n_ctx_tiles=mix_ctx_tiles)
        w_up = ffn_w_up[i].astype(BF16)
        gate = _ffn_gate(h, mod, row(norm2_g[i]), w_up, batch=batch, n_ctx_tiles=ffn_ctx_tiles, d_ff=d_ff)
        h = _ffn(h, mod, row(norm2_g[i]), gate, w_up, ffn_w_down[i].astype(BF16),
                 ffn_dw[i].reshape(9, d_ff), row(ffn_dw_b[i]), batch=batch, n_ctx_tiles=ffn_ctx_tiles, d_ff=d_ff)

    out = _final_norm(h, row(final_g), first_row=ctx_len * batch, out_rows=seq * batch)
    return jnp.swapaxes(out.reshape(seq, batch, d), 0, 1)
```

```python
import functools

import jax
import jax.numpy as jnp
from jax import lax
from jax.experimental import pallas as pl
from jax.experimental.pallas import tpu as pltpu

GRID_W = 64
S5_GROUP = 16
S5_STATE = 64
N_MOD = 6
EPS = 1e-6
LANES = 128
S5_BLOCK_GROUPS = LANES // S5_GROUP
MIX_STEPS = 32
CONV_HALO_STEPS = 16
FFN_BLOCK = 256
FFN_ROW_BLOCK = 256
MXU_N = 256
VMEM_LIMIT = 56 * 1024 * 1024
FFN_VMEM_LIMIT = 60 * 1024 * 1024

BF16 = jnp.bfloat16
F32 = jnp.float32


def _dot(a, b):
    return jnp.dot(a, b, preferred_element_type=F32)


def _split_bf16(a):
    hi = a.astype(BF16)
    lo = (a - hi.astype(F32)).astype(BF16)
    return hi, lo


def _dot_3pass(a, b):
    a_hi, a_lo = _split_bf16(a)
    b_hi, b_lo = _split_bf16(b)
    return _dot(a_hi, b_hi) + (_dot(a_lo, b_hi) + _dot(a_hi, b_lo))


def _params(*sem):
    return pltpu.CompilerParams(dimension_semantics=sem, vmem_limit_bytes=VMEM_LIMIT)


def _norm_modulate(x, gain, mod_ref, batch, k_shift, k_scale):
    rows, d = x.shape
    ms = jnp.mean(x * x, axis=-1, keepdims=True)
    n = (x * lax.rsqrt(ms + EPS)) * gain
    shift = mod_ref[:, k_shift * d:(k_shift + 1) * d]
    scale = mod_ref[:, k_scale * d:(k_scale + 1) * d]
    n3 = n.reshape(rows // batch, batch, d)
    out = n3 * (1.0 + scale)[None] + shift[None]
    return out.reshape(rows, d)


def _gated_residual(x, y, mod_ref, batch, k_gate):
    rows, d = x.shape
    gate = mod_ref[:, k_gate * d:(k_gate + 1) * d]
    out = x.reshape(rows // batch, batch, d) + gate[None] * y.reshape(rows // batch, batch, d)
    return out.reshape(rows, d)


def _mod_kernel(c_ref, w_ref, b_ref, o_ref):
    c = c_ref[...]
    s = c * jax.nn.sigmoid(c)
    o_ref[...] = _dot_3pass(s, w_ref[...]) + b_ref[...]


def _modulation(cc, ada_w, ada_b):
    depth, d, n = ada_w.shape
    rows = cc.shape[0]
    tn = d
    return pl.pallas_call(
        _mod_kernel,
        grid=(depth, n // tn),
        in_specs=[pl.BlockSpec((rows, d), lambda l, j: (0, 0)),
                  pl.BlockSpec((None, d, tn), lambda l, j: (l, 0, j)),
                  pl.BlockSpec((None, 1, tn), lambda l, j: (l, 0, j))],
        out_specs=pl.BlockSpec((None, rows, tn), lambda l, j: (l, 0, j)),
        out_shape=jax.ShapeDtypeStruct((depth, rows, n), F32),
        compiler_params=_params("arbitrary", "arbitrary"),
        name="adaln_mod",
    )(cc, ada_w, ada_b.reshape(depth, 1, n))


def _disc_kernel(lr_ref, li_ref, ldt_ref, br_ref, bi_ref, ar_ref, ai_ref, bbr_ref, bbi_ref):
    dt = jnp.exp(ldt_ref[...])
    lr = lr_ref[...]
    li = li_ref[...]
    mag = jnp.exp(dt * lr)
    ab_re = mag * jnp.cos(dt * li)
    ab_im = mag * jnp.sin(dt * li)
    den = lr * lr + li * li
    nr = ab_re - 1.0
    k_re = (nr * lr + ab_im * li) / den
    k_im = (ab_im * lr - nr * li) / den
    br = br_ref[...]
    bi = bi_ref[...]
    ar_ref[...] = ab_re
    ai_ref[...] = ab_im
    bbr_ref[...] = k_re * br - k_im * bi
    bbi_ref[...] = k_re * bi + k_im * br


def _discretise(lam_re, lam_im, log_dt, b_re, b_im):
    lead = lam_re.shape[:-1]
    p = lam_re.shape[-1]
    h = b_re.shape[-1]
    r = 1
    for s in lead:
        r *= s
    lr = lam_re.reshape(r, 1, p)
    li = lam_im.reshape(r, 1, p)
    ldt = jnp.broadcast_to(log_dt.reshape(r, 1, 1), (r, 1, p))
    br = jnp.swapaxes(b_re, -1, -2).reshape(r, h, p)
    bi = jnp.swapaxes(b_im, -1, -2).reshape(r, h, p)
    full = lambda shape: pl.BlockSpec(shape, lambda i: (0,) * len(shape))
    ar, ai, bbr, bbi = pl.pallas_call(
        _disc_kernel,
        grid=(1,),
        in_specs=[full((r, 1, p)), full((r, 1, p)), full((r, 1, p)), full((r, h, p)), full((r, h, p))],
        out_specs=[full((r, 1, p)), full((r, 1, p)), full((r, h, p)), full((r, h, p))],
        out_shape=[jax.ShapeDtypeStruct((r, 1, p), F32), jax.ShapeDtypeStruct((r, 1, p), F32),
                   jax.ShapeDtypeStruct((r, h, p), F32), jax.ShapeDtypeStruct((r, h, p), F32)],
        compiler_params=_params("arbitrary"),
        name="s5_discretise",
    )(lr, li, ldt, br, bi)
    return (ar.reshape(lead + (p,)), ai.reshape(lead + (p,)),
            bbr.reshape(lead + (h, p)), bbi.reshape(lead + (h, p)))


def _block_diag(m):
    g = m.shape[-3]
    eye = jnp.eye(g, dtype=m.dtype)
    out = m[..., :, :, None, :] * eye[:, None, :, None]
    return out.reshape(m.shape[:-3] + (g * m.shape[-2], g * m.shape[-1]))


def _s5_operands(ar, ai, bbr, bbi, c_re, c_im):
    lead = ar.shape[:-2]
    g, p = ar.shape[-2:]
    h = bbr.shape[-2]
    nb = g // S5_BLOCK_GROUPS
    bg = S5_BLOCK_GROUPS
    blk = lambda m: m.astype(BF16).reshape(lead + (nb, bg) + m.shape[-2:])
    b_in = jnp.concatenate([_block_diag(blk(bbr)), _block_diag(blk(bbi))], axis=-1)
    ct = lambda m: jnp.swapaxes(m, -1, -2)
    c_out = jnp.concatenate([_block_diag(blk(ct(c_re))), _block_diag(blk(ct(-c_im)))], axis=-2)
    a_re = ar.reshape(lead + (nb, 1, bg * p))
    a_im = ai.reshape(lead + (nb, 1, bg * p))
    return b_in, c_out, a_re, a_im


def _inproj_kernel(h_ref, mod_ref, g_ref, w_ref, u_ref, a_ref, *, batch, d_s5, d_conv):
    nl = _norm_modulate(h_ref[...], g_ref[...], mod_ref, batch, 0, 1).astype(BF16)
    u_ref[...] = _dot(nl, w_ref[:, :d_s5]).astype(BF16)
    off = d_s5
    v1 = _dot(nl, w_ref[:, off:off + d_conv])
    v2 = _dot(nl, w_ref[:, off + d_conv:off + 2 * d_conv])
    a_ref[...] = (v1 * jax.nn.sigmoid(v2)).astype(BF16)


def _inproj(h, mod, gain, w_in, *, batch, n_ctx_tiles, d_s5, d_conv):
    rows, d = h.shape
    tr = GRID_W * batch
    d_in = w_in.shape[1]
    kind = lambda i: (jnp.where(i >= n_ctx_tiles, 1, 0), 0, 0)
    return pl.pallas_call(
        functools.partial(_inproj_kernel, batch=batch, d_s5=d_s5, d_conv=d_conv),
        grid=(rows // tr,),
        in_specs=[pl.BlockSpec((tr, d), lambda i: (i, 0)),
                  pl.BlockSpec((None, batch, N_MOD * d), kind),
                  pl.BlockSpec((1, d), lambda i: (0, 0)),
                  pl.BlockSpec((d, d_in), lambda i: (0, 0))],
        out_specs=[pl.BlockSpec((tr, d_s5), lambda i: (i, 0)),
                   pl.BlockSpec((tr, d_conv), lambda i: (i, 0))],
        out_shape=[jax.ShapeDtypeStruct((rows, d_s5), BF16),
                   jax.ShapeDtypeStruct((rows, d_conv), BF16)],
        compiler_params=_params("arbitrary"),
        name="mix_inproj",
    )(h, mod, gain, w_in)


def _s5_kernel(uf0_ref, uf1_ref, uf2_ref, ub0_ref, ub1_ref, ub2_ref, bin_ref, cout_ref, are_ref, aim_ref,
               yf_ref, yb_ref, bufs, carry, *, batch, steps):
    m = pl.program_id(1)
    ns = are_ref.shape[-1]
    tr = steps * batch
    u_first = (uf0_ref, ub0_ref)
    u_odd = (uf1_ref, ub1_ref)
    u_next = (uf2_ref, ub2_ref)
    y_refs = (yf_ref, yb_ref)

    def project_in(u_ref, direction, slot):
        bufs[direction, slot] = _dot(u_ref[...], bin_ref[direction])

    def recur(direction, slot):
        a_re = jnp.broadcast_to(are_ref[direction], (batch, ns))
        a_im = jnp.broadcast_to(aim_ref[direction], (batch, ns))
        h_re = carry[direction, :, 0:ns]
        h_im = carry[direction, :, ns:2 * ns]
        for s in range(steps):
            t = (steps - 1 - s) if direction == 1 else s
            rows = slice(t * batch, (t + 1) * batch)
            n_re = a_re * h_re - a_im * h_im + bufs[direction, slot, rows, 0:ns]
            n_im = a_re * h_im + a_im * h_re + bufs[direction, slot, rows, ns:2 * ns]
            bufs[direction, slot, rows, 0:ns] = n_re
            bufs[direction, slot, rows, ns:2 * ns] = n_im
            h_re, h_im = n_re, n_im
        carry[direction, :, 0:ns] = h_re
        carry[direction, :, ns:2 * ns] = h_im

    def project_out(direction, slot, position):
        half = position if direction == 0 else 1 - position
        y = _dot(bufs[direction, slot].astype(BF16), cout_ref[direction])
        y_refs[direction][half * tr:(half + 1) * tr, :] = y.astype(BF16)

    @pl.when(m == 0)
    def _():
        carry[...] = jnp.zeros_like(carry)
        for direction in range(2):
            project_in(u_first[direction], direction, 0)

    for direction in range(2):
        project_in(u_odd[direction], direction, 1)
        recur(direction, 0)
    for direction in range(2):
        project_out(direction, 0, 0)
        recur(direction, 1)
    for direction in range(2):
        project_in(u_next[direction], direction, 0)
        project_out(direction, 1, 1)


def _s5_scan(u, b_in, c_out, a_re, a_im, *, batch, n_ctx_tiles):
    rows, d_s5 = u.shape
    nb = b_in.shape[1]
    ns = a_re.shape[-1]
    tr = MIX_STEPS * batch
    nt = rows // tr
    assert nt % 2 == 0 and n_ctx_tiles % 2 == 0

    def rev(p):
        return jnp.where(p < n_ctx_tiles, n_ctx_tiles - 1 - p, nt - 1 + n_ctx_tiles - p)

    nxt = lambda m: jnp.minimum(2 * m + 2, nt - 1)
    chunk = lambda index: pl.BlockSpec((tr, LANES), index)
    param = lambda shape: pl.BlockSpec((2, None) + shape, lambda j, m: (0, j, 0, 0))
    return pl.pallas_call(
        functools.partial(_s5_kernel, batch=batch, steps=MIX_STEPS),
        grid=(nb, nt // 2),
        in_specs=[chunk(lambda j, m: (0, j)),
                  chunk(lambda j, m: (2 * m + 1, j)),
                  chunk(lambda j, m: (nxt(m), j)),
                  chunk(lambda j, m: (rev(0), j)),
                  chunk(lambda j, m: (rev(2 * m + 1), j)),
                  chunk(lambda j, m: (rev(nxt(m)), j)),
                  param((LANES, 2 * ns)), param((2 * ns, LANES)), param((1, ns)), param((1, ns))],
        out_specs=[pl.BlockSpec((2 * tr, LANES), lambda j, m: (m, j)),
                   pl.BlockSpec((2 * tr, LANES), lambda j, m: (rev(2 * m + 1) // 2, j))],
        out_shape=[jax.ShapeDtypeStruct((rows, d_s5), BF16), jax.ShapeDtypeStruct((rows, d_s5), BF16)],
        scratch_shapes=[pltpu.VMEM((2, 2, tr, 2 * ns), F32), pltpu.VMEM((2, batch, 2 * ns), F32)],
        compiler_params=_params("arbitrary", "arbitrary"),
        name="s5_scan",
    )(u, u, u, u, u, u, b_in, c_out, a_re, a_im)


def _mixout_kernel(h_ref, mod_ref, g_ref, u_ref, yf_ref, yb_ref, ac_ref, ap_ref, an_ref, wgate_ref,
                   sd_ref, wglu_ref, wa_ref, dw_ref, dwb_ref, lng_ref, lnb_ref, wb_ref, wout_ref,
                   o_ref, aext, conv, *, batch, n_ctx_tiles, n_tiles):
    i = pl.program_id(0)
    tr, d = h_ref.shape
    halo = ap_ref.shape[0]
    taps = dw_ref.shape[0]

    ys = sd_ref[...] * u_ref[...].astype(F32) + yf_ref[...].astype(F32) + yb_ref[...].astype(F32)
    g = jax.nn.gelu(ys)
    ya = _dot((g * jax.nn.sigmoid(_dot(g.astype(BF16), wglu_ref[...]))).astype(BF16), wa_ref[...])

    nl = _norm_modulate(h_ref[...], g_ref[...], mod_ref, batch, 0, 1).astype(BF16)
    gate_a = jax.nn.sigmoid(_dot(nl, wgate_ref[:, 0:d]))
    gate_b = jax.nn.sigmoid(_dot(nl, wgate_ref[:, d:2 * d]))

    first = jnp.logical_or(i == 0, i == n_ctx_tiles)
    last = jnp.logical_or(i == n_ctx_tiles - 1, i == n_tiles - 1)
    aext[0:halo] = jnp.where(first, 0.0, ap_ref[...].astype(F32))
    aext[halo:halo + tr] = ac_ref[...].astype(F32)
    aext[halo + tr:2 * halo + tr] = jnp.where(last, 0.0, an_ref[...].astype(F32))
    chunk = 2 * batch

    for c in range(tr // chunk):
        r0 = c * chunk
        acc = jnp.broadcast_to(dwb_ref[...], (chunk, dwb_ref.shape[1]))
        for k in range(taps):
            off = r0 + halo + (k - taps // 2) * batch
            acc = acc + dw_ref[k:k + 1, :] * aext[off:off + chunk, :]
        conv[r0:r0 + chunk, :] = acc
    a = conv[...]
    mu = jnp.mean(a, axis=-1, keepdims=True)
    ac = a - mu
    var = jnp.mean(ac * ac, axis=-1, keepdims=True)
    ln = (ac * lax.rsqrt(var + EPS)) * lng_ref[...] + lnb_ref[...]
    yb = _dot((ln * jax.nn.sigmoid(ln)).astype(BF16), wb_ref[...])

    merged = gate_a * ya + gate_b * yb
    y = _dot(merged.astype(BF16), wout_ref[...])
    o_ref[...] = _gated_residual(h_ref[...], y, mod_ref, batch, 2)


def _mixout(h, mod, gain, u, yf, yb, a, w_gate, s5_d, w_glu, w_a, conv_dw, conv_dw_b, ln_g, ln_b, w_b, w_out,
            *, batch, n_ctx_tiles):
    rows, d = h.shape
    d_s5 = u.shape[1]
    d_conv = a.shape[1]
    tr = MIX_STEPS * batch
    halo = CONV_HALO_STEPS * batch
    ratio = tr // halo
    nt = rows // tr
    n_halo_blocks = rows // halo
    taps = conv_dw.shape[0]
    assert taps // 2 <= CONV_HALO_STEPS
    kind = lambda i: (jnp.where(i >= n_ctx_tiles, 1, 0), 0, 0)
    row_tile = lambda w: pl.BlockSpec((tr, w), lambda i: (i, 0))
    const = lambda s: pl.BlockSpec(s, lambda i: (0,) * len(s))
    return pl.pallas_call(
        functools.partial(_mixout_kernel, batch=batch, n_ctx_tiles=n_ctx_tiles, n_tiles=nt),
        grid=(nt,),
        in_specs=[row_tile(d),
                  pl.BlockSpec((None, batch, N_MOD * d), kind),
                  const((1, d)),
                  row_tile(d_s5), row_tile(d_s5), row_tile(d_s5),
                  row_tile(d_conv),
                  pl.BlockSpec((halo, d_conv), lambda i: (jnp.maximum(i * ratio - 1, 0), 0)),
                  pl.BlockSpec((halo, d_conv), lambda i: (jnp.minimum((i + 1) * ratio, n_halo_blocks - 1), 0)),
                  const((d, 2 * d)),
                  const((1, d_s5)), const((d_s5, d_s5)), const((d_s5, d)),
                  const((taps, d_conv)), const((1, d_conv)), const((1, d_conv)), const((1, d_conv)),
                  const((d_conv, d)), const((d, d))],
        out_specs=row_tile(d),
        out_shape=jax.ShapeDtypeStruct((rows, d), F32),
        scratch_shapes=[pltpu.VMEM((tr + 2 * halo, d_conv), F32), pltpu.VMEM((tr, d_conv), F32)],
        compiler_params=_params("arbitrary"),
        name="mix_out",
    )(h, mod, gain, u, yf, yb, a, a, a, w_gate, s5_d, w_glu, w_a, conv_dw, conv_dw_b, ln_g, ln_b, w_b, w_out)


def _ffn_gate_kernel(h_ref, mod_ref, g_ref, w_ref, o_ref, *, batch):
    nl = _norm_modulate(h_ref[...], g_ref[...], mod_ref, batch, 3, 4).astype(BF16)
    d_ff = o_ref.shape[1]
    for c in range(d_ff // FFN_BLOCK):
        cols = slice(c * FFN_BLOCK, (c + 1) * FFN_BLOCK)
        o_ref[:, cols] = _dot(nl, w_ref[:, cols]).astype(BF16)


def _ffn_gate(h, mod, gain, w_up, *, batch, n_ctx_tiles, d_ff):
    rows, d = h.shape
    tr = GRID_W * batch
    kind = lambda i: (jnp.where(i >= n_ctx_tiles, 1, 0), 0, 0)
    return pl.pallas_call(
        functools.partial(_ffn_gate_kernel, batch=batch),
        grid=(rows // tr,),
        in_specs=[pl.BlockSpec((tr, d), lambda i: (i, 0)),
                  pl.BlockSpec((None, batch, N_MOD * d), kind),
                  pl.BlockSpec((1, d), lambda i: (0, 0)),
                  pl.BlockSpec((d, d_ff), lambda i: (0, 0))],
        out_specs=pl.BlockSpec((tr, d_ff), lambda i: (i, 0)),
        out_shape=jax.ShapeDtypeStruct((rows, d_ff), BF16),
        compiler_params=_params("arbitrary"),
        name="ffn_gate",
    )(h, mod, gain, w_up)


def _ffn_kernel(h_ref, mod_ref, g_ref, gp_ref, gc_ref, gn_ref, wv_ref, wd_ref, dw_ref, db_ref,
                o_ref, nl_ref, act_ref, *, batch, n_ctx_tiles, n_tiles):
    i = pl.program_id(0)
    f = pl.program_id(1)
    nf = pl.num_programs(1)
    tr, d = h_ref.shape
    fw = gc_ref.shape[1]
    ksz = 3

    @pl.when(f == 0)
    def _():
        nl_ref[...] = _norm_modulate(h_ref[...], g_ref[...], mod_ref, batch, 3, 4).astype(BF16)
        o_ref[...] = jnp.zeros_like(o_ref)

    is_ctx = i < n_ctx_tiles
    up_ok = jnp.logical_and(jnp.logical_not(is_ctx), i >= n_ctx_tiles + 1)
    down_ok = jnp.logical_and(jnp.logical_not(is_ctx), i <= n_tiles - 2)
    ctx_lo = jnp.logical_and(is_ctx, i >= 1)
    ctx_hi = jnp.logical_and(is_ctx, i <= n_ctx_tiles - 2)
    row_ok = (up_ok, None, down_ok)
    srcs = (gp_ref, gc_ref, gn_ref)
    group = FFN_ROW_BLOCK // batch
    n_blocks = tr // FFN_ROW_BLOCK

    def conv_block(rb):
        r0 = rb * FFN_ROW_BLOCK
        for c in range(fw // LANES):
            cols = slice(c * LANES, (c + 1) * LANES)
            w = []
            for s in range(ksz):
                for k in range(ksz):
                    wk = dw_ref[s * ksz + k:s * ksz + k + 1, cols]
                    if row_ok[s] is not None:
                        wk = jnp.where(row_ok[s], wk, 0.0)
                    w.append(jnp.broadcast_to(wk.astype(BF16), (batch, LANES)))
            bias = jnp.broadcast_to(db_ref[:, cols].astype(BF16), (batch, LANES))

            def step_rows(s, rj):
                zero = jnp.zeros((batch, LANES), BF16)
                if rj < 0:
                    return jnp.where(ctx_lo, gp_ref[tr - batch:tr, cols], zero) if s == 1 else None
                if rj >= tr:
                    return jnp.where(ctx_hi, gn_ref[0:batch, cols], zero) if s == 1 else None
                return srcs[s][rj:rj + batch, cols]

            window = [[step_rows(s, r0 + j * batch) for j in (-1, 0)] for s in range(ksz)]
            for j in range(group):
                rj = r0 + j * batch
                acc = bias
                for s in range(ksz):
                    window[s].append(step_rows(s, rj + batch))
                    for k in range(ksz):
                        if window[s][k] is not None:
                            acc = acc + w[s * ksz + k] * window[s][k]
                    window[s].pop(0)
                act_ref[rj:rj + batch, cols] = acc * jax.nn.sigmoid(acc)

    def project_block(rb):
        rows = slice(rb * FFN_ROW_BLOCK, (rb + 1) * FFN_ROW_BLOCK)
        nl = nl_ref[rows, :]
        for c0 in range(0, fw, MXU_N):
            cols = slice(c0, min(c0 + MXU_N, fw))
            act_ref[rows, cols] = (act_ref[rows, cols].astype(F32) * _dot(nl, wv_ref[:, cols])).astype(BF16)
        o_ref[rows, :] += _dot(act_ref[rows, :], wd_ref[...])

    conv_block(0)
    for rb in range(n_blocks - 1):
        conv_block(rb + 1)
        project_block(rb)
    project_block(n_blocks - 1)

    @pl.when(f == nf - 1)
    def _():
        o_ref[...] = _gated_residual(h_ref[...], o_ref[...], mod_ref, batch, 5)


def _ffn_block(d_ff):
    half = d_ff // 2
    return half if half % LANES == 0 else d_ff


def _ffn(h, mod, gain, gate, w_up, w_down, dw, dw_b, *, batch, n_ctx_tiles, d_ff):
    rows, d = h.shape
    tr = GRID_W * batch
    nt = rows // tr
    fw = _ffn_block(d_ff)
    nf = d_ff // fw
    kind = lambda i, f: (jnp.where(i >= n_ctx_tiles, 1, 0), 0, 0)
    return pl.pallas_call(
        functools.partial(_ffn_kernel, batch=batch, n_ctx_tiles=n_ctx_tiles, n_tiles=nt),
        grid=(nt, nf),
        in_specs=[pl.BlockSpec((tr, d), lambda i, f: (i, 0)),
                  pl.BlockSpec((None, batch, N_MOD * d), kind),
                  pl.BlockSpec((1, d), lambda i, f: (0, 0)),
                  pl.BlockSpec((tr, fw), lambda i, f: (jnp.maximum(i - 1, 0), f)),
                  pl.BlockSpec((tr, fw), lambda i, f: (i, f)),
                  pl.BlockSpec((tr, fw), lambda i, f: (jnp.minimum(i + 1, nt - 1), f)),
                  pl.BlockSpec((d, fw), lambda i, f: (0, nf + f)),
                  pl.BlockSpec((fw, d), lambda i, f: (f, 0)),
                  pl.BlockSpec((9, fw), lambda i, f: (0, f)),
                  pl.BlockSpec((1, fw), lambda i, f: (0, f))],
        out_specs=pl.BlockSpec((tr, d), lambda i, f: (i, 0)),
        out_shape=jax.ShapeDtypeStruct((rows, d), F32),
        scratch_shapes=[pltpu.VMEM((tr, d), BF16), pltpu.VMEM((tr, fw), BF16)],
        compiler_params=pltpu.CompilerParams(dimension_semantics=("arbitrary", "arbitrary"),
                                             vmem_limit_bytes=FFN_VMEM_LIMIT),
        name="ffn_main",
    )(h, mod, gain, gate, gate, gate, w_up, w_down, dw, dw_b)


def _final_norm_kernel(h_ref, g_ref, o_ref):
    x = h_ref[...]
    ms = jnp.mean(x * x, axis=-1, keepdims=True)
    o_ref[...] = (x * lax.rsqrt(ms + EPS)) * g_ref[...]


def _final_norm(h, gain, *, first_row, out_rows):
    d = h.shape[1]
    tr = 1024
    off = first_row // tr
    return pl.pallas_call(
        _final_norm_kernel,
        grid=(out_rows // tr,),
        in_specs=[pl.BlockSpec((tr, d), lambda i: (i + off, 0)),
                  pl.BlockSpec((1, d), lambda i: (0, 0))],
        out_specs=pl.BlockSpec((tr, d), lambda i: (i, 0)),
        out_shape=jax.ShapeDtypeStruct((out_rows, d), F32),
        compiler_params=_params("arbitrary"),
        name="final_norm",
    )(h, gain)


def kernel(x, c, ctx, c_ctx, ada_w, ada_b, norm1_g, w_in, s5_lam_re, s5_lam_im, s5_log_dt, s5_b_re, s5_b_im, s5_c_re, s5_c_im, s5_d, w_glu, w_a, conv_dw, conv_dw_b, conv_ln_g, conv_ln_b, w_b, w_out, norm2_g, ffn_w_up, ffn_dw, ffn_dw_b, ffn_w_down, final_g):
    batch, seq, d = x.shape
    ctx_len = ctx.shape[1]
    depth = ada_w.shape[0]
    d_s5 = w_glu.shape[1]
    d_conv = conv_dw.shape[2]
    d_ff = ffn_dw.shape[-1]
    assert batch % 16 == 0 and seq % GRID_W == 0 and ctx_len % GRID_W == 0
    assert d_s5 % LANES == 0 and d_ff % FFN_BLOCK == 0
    mix_ctx_tiles = ctx_len // MIX_STEPS
    ffn_ctx_tiles = ctx_len // GRID_W

    h = jnp.concatenate([jnp.swapaxes(ctx, 0, 1), jnp.swapaxes(x, 0, 1)], axis=0).reshape(-1, d)

    cc = jnp.concatenate([jnp.broadcast_to(c_ctx[None], (batch, d)), c], axis=0)
    mods = _modulation(cc, ada_w, ada_b).reshape(depth, 2, batch, N_MOD * d)

    ar, ai, bbr, bbi = _discretise(s5_lam_re, s5_lam_im, s5_log_dt, s5_b_re, s5_b_im)
    b_in, c_out, a_re, a_im = _s5_operands(ar, ai, bbr, bbi, s5_c_re, s5_c_im)

    row = lambda v: v.reshape(1, -1)
    for i in range(depth):
        mod = mods[i]
        off_gate = d_s5 + 2 * d_conv
        u, a = _inproj(h, mod, row(norm1_g[i]), w_in[i, :, :off_gate].astype(BF16), batch=batch,
                       n_ctx_tiles=ffn_ctx_tiles, d_s5=d_s5, d_conv=d_conv)
        yf, yb = _s5_scan(u, b_in[i], c_out[i], a_re[i], a_im[i], batch=batch, n_ctx_tiles=mix_ctx_tiles)
        h = _mixout(h, mod, row(norm1_g[i]), u, yf, yb, a, w_in[i, :, off_gate:].astype(BF16),
                    row(s5_d[i]), w_glu[i].astype(BF16), w_a[i].astype(BF16),
                    conv_dw[i], row(conv_dw_b[i]), row(conv_ln_g[i]), row(conv_ln_b[i]),
                    w_b[i].astype(BF16), w_out[i].astype(BF16), batch=batch, n_ctx_tiles=mix_ctx_tiles)
        w_up = ffn_w_up[i].astype(BF16)
        gate = _ffn_gate(h, mod, row(norm2_g[i]), w_up, batch=batch, n_ctx_tiles=ffn_ctx_tiles, d_ff=d_ff)
        h = _ffn(h, mod, row(norm2_g[i]), gate, w_up, ffn_w_down[i].astype(BF16),
                 ffn_dw[i].reshape(9, d_ff), row(ffn_dw_b[i]), batch=batch, n_ctx_tiles=ffn_ctx_tiles, d_ff=d_ff)

    out = _final_norm(h, row(final_g), first_row=ctx_len * batch, out_rows=seq * batch)
    return jnp.swapaxes(out.reshape(seq, batch, d), 0, 1)
```

```python
import functools

import jax
import jax.numpy as jnp
from jax import lax
from jax.experimental import pallas as pl
from jax.experimental.pallas import tpu as pltpu

GRID_W = 64
S5_GROUP = 16
S5_STATE = 64
N_MOD = 6
EPS = 1e-6
LANES = 128
S5_BLOCK_GROUPS = LANES // S5_GROUP
MIX_STEPS = 32
CONV_HALO_STEPS = 16
FFN_BLOCK = 256
FFN_ROW_BLOCK = 256
MXU_N = 256
VMEM_LIMIT = 56 * 1024 * 1024
FFN_VMEM_LIMIT = 60 * 1024 * 1024

BF16 = jnp.bfloat16
F32 = jnp.float32


def _dot(a, b):
    return jnp.dot(a, b, preferred_element_type=F32)


def _split_bf16(a):
    hi = a.astype(BF16)
    lo = (a - hi.astype(F32)).astype(BF16)
    return hi, lo


def _dot_3pass(a, b):
    a_hi, a_lo = _split_bf16(a)
    b_hi, b_lo = _split_bf16(b)
    return _dot(a_hi, b_hi) + (_dot(a_lo, b_hi) + _dot(a_hi, b_lo))


def _params(*sem):
    return pltpu.CompilerParams(dimension_semantics=sem, vmem_limit_bytes=VMEM_LIMIT)


def _norm_modulate(x, gain, mod_ref, batch, k_shift, k_scale):
    rows, d = x.shape
    ms = jnp.mean(x * x, axis=-1, keepdims=True)
    n = (x * lax.rsqrt(ms + EPS)) * gain
    shift = mod_ref[:, k_shift * d:(k_shift + 1) * d]
    scale = mod_ref[:, k_scale * d:(k_scale + 1) * d]
    n3 = n.reshape(rows // batch, batch, d)
    out = n3 * (1.0 + scale)[None] + shift[None]
    return out.reshape(rows, d)


def _gated_residual(x, y, mod_ref, batch, k_gate):
    rows, d = x.shape
    gate = mod_ref[:, k_gate * d:(k_gate + 1) * d]
    out = x.reshape(rows // batch, batch, d) + gate[None] * y.reshape(rows // batch, batch, d)
    return out.reshape(rows, d)


def _mod_kernel(c_ref, w_ref, b_ref, o_ref):
    c = c_ref[...]
    s = c * jax.nn.sigmoid(c)
    o_ref[...] = _dot_3pass(s, w_ref[...]) + b_ref[...]


def _modulation(cc, ada_w, ada_b):
    depth, d, n = ada_w.shape
    rows = cc.shape[0]
    tn = d
    return pl.pallas_call(
        _mod_kernel,
        grid=(depth, n // tn),
        in_specs=[pl.BlockSpec((rows, d), lambda l, j: (0, 0)),
                  pl.BlockSpec((None, d, tn), lambda l, j: (l, 0, j)),
                  pl.BlockSpec((None, 1, tn), lambda l, j: (l, 0, j))],
        out_specs=pl.BlockSpec((None, rows, tn), lambda l, j: (l, 0, j)),
        out_shape=jax.ShapeDtypeStruct((depth, rows, n), F32),
        compiler_params=_params("arbitrary", "arbitrary"),
        name="adaln_mod",
    )(cc, ada_w, ada_b.reshape(depth, 1, n))


def _disc_kernel(lr_ref, li_ref, ldt_ref, br_ref, bi_ref, ar_ref, ai_ref, bbr_ref, bbi_ref):
    dt = jnp.exp(ldt_ref[...])
    lr = lr_ref[...]
    li = li_ref[...]
    mag = jnp.exp(dt * lr)
    ab_re = mag * jnp.cos(dt * li)
    ab_im = mag * jnp.sin(dt * li)
    den = lr * lr + li * li
    nr = ab_re - 1.0
    k_re = (nr * lr + ab_im * li) / den
    k_im = (ab_im * lr - nr * li) / den
    br = br_ref[...]
    bi = bi_ref[...]
    ar_ref[...] = ab_re
    ai_ref[...] = ab_im
    bbr_ref[...] = k_re * br - k_im * bi
    bbi_ref[...] = k_re * bi + k_im * br


def _discretise(lam_re, lam_im, log_dt, b_re, b_im):
    lead = lam_re.shape[:-1]
    p = lam_re.shape[-1]
    h = b_re.shape[-1]
    r = 1
    for s in lead:
        r *= s
    lr = lam_re.reshape(r, 1, p)
    li = lam_im.reshape(r, 1, p)
    ldt = jnp.broadcast_to(log_dt.reshape(r, 1, 1), (r, 1, p))
    br = jnp.swapaxes(b_re, -1, -2).reshape(r, h, p)
    bi = jnp.swapaxes(b_im, -1, -2).reshape(r, h, p)
    full = lambda shape: pl.BlockSpec(shape, lambda i: (0,) * len(shape))
    ar, ai, bbr, bbi = pl.pallas_call(
        _disc_kernel,
        grid=(1,),
        in_specs=[full((r, 1, p)), full((r, 1, p)), full((r, 1, p)), full((r, h, p)), full((r, h, p))],
        out_specs=[full((r, 1, p)), full((r, 1, p)), full((r, h, p)), full((r, h, p))],
        out_shape=[jax.ShapeDtypeStruct((r, 1, p), F32), jax.ShapeDtypeStruct((r, 1, p), F32),
                   jax.ShapeDtypeStruct((r, h, p), F32), jax.ShapeDtypeStruct((r, h, p), F32)],
        compiler_params=_params("arbitrary"),
        name="s5_discretise",
    )(lr, li, ldt, br, bi)
    return (ar.reshape(lead + (p,)), ai.reshape(lead + (p,)),
            bbr.reshape(lead + (h, p)), bbi.reshape(lead + (h, p)))


def _block_diag(m):
    g = m.shape[-3]
    eye = jnp.eye(g, dtype=m.dtype)
    out = m[..., :, :, None, :] * eye[:, None, :, None]
    return out.reshape(m.shape[:-3] + (g * m.shape[-2], g * m.shape[-1]))


def _s5_operands(ar, ai, bbr, bbi, c_re, c_im):
    lead = ar.shape[:-2]
    g, p = ar.shape[-2:]
    h = bbr.shape[-2]
    nb = g // S5_BLOCK_GROUPS
    bg = S5_BLOCK_GROUPS
    blk = lambda m: m.astype(BF16).reshape(lead + (nb, bg) + m.shape[-2:])
    b_in = jnp.concatenate([_block_diag(blk(bbr)), _block_diag(blk(bbi))], axis=-1)
    ct = lambda m: jnp.swapaxes(m, -1, -2)
    c_out = jnp.concatenate([_block_diag(blk(ct(c_re))), _block_diag(blk(ct(-c_im)))], axis=-2)
    a_re = ar.reshape(lead + (nb, 1, bg * p))
    a_im = ai.reshape(lead + (nb, 1, bg * p))
    return b_in, c_out, a_re, a_im


def _inproj_kernel(h_ref, mod_ref, g_ref, w_ref, u_ref, a_ref, *, batch, d_s5, d_conv):
    nl = _norm_modulate(h_ref[...], g_ref[...], mod_ref, batch, 0, 1).astype(BF16)
    u_ref[...] = _dot(nl, w_ref[:, :d_s5]).astype(BF16)
    off = d_s5
    v1 = _dot(nl, w_ref[:, off:off + d_conv])
    v2 = _dot(nl, w_ref[:, off + d_conv:off + 2 * d_conv])
    a_ref[...] = (v1 * jax.nn.sigmoid(v2)).astype(BF16)


def _inproj(h, mod, gain, w_in, *, batch, n_ctx_tiles, d_s5, d_conv):
    rows, d = h.shape
    tr = GRID_W * batch
    d_in = w_in.shape[1]
    kind = lambda i: (jnp.where(i >= n_ctx_tiles, 1, 0), 0, 0)
    return pl.pallas_call(
        functools.partial(_inproj_kernel, batch=batch, d_s5=d_s5, d_conv=d_conv),
        grid=(rows // tr,),
        in_specs=[pl.BlockSpec((tr, d), lambda i: (i, 0)),
                  pl.BlockSpec((None, batch, N_MOD * d), kind),
                  pl.BlockSpec((1, d), lambda i: (0, 0)),
                  pl.BlockSpec((d, d_in), lambda i: (0, 0))],
        out_specs=[pl.BlockSpec((tr, d_s5), lambda i: (i, 0)),
                   pl.BlockSpec((tr, d_conv), lambda i: (i, 0))],
        out_shape=[jax.ShapeDtypeStruct((rows, d_s5), BF16),
                   jax.ShapeDtypeStruct((rows, d_conv), BF16)],
        compiler_params=_params("arbitrary"),
        name="mix_inproj",
    )(h, mod, gain, w_in)


def _s5_kernel(uf0_ref, uf1_ref, uf2_ref, ub0_ref, ub1_ref, ub2_ref, bin_ref, cout_ref, are_ref, aim_ref,
               yf_ref, yb_ref, bufs, carry, *, batch, steps):
    m = pl.program_id(1)
    ns = are_ref.shape[-1]
    tr = steps * batch
    u_first = (uf0_ref, ub0_ref)
    u_odd = (uf1_ref, ub1_ref)
    u_next = (uf2_ref, ub2_ref)
    y_refs = (yf_ref, yb_ref)

    def project_in(u_ref, direction, slot):
        bufs[direction, slot] = _dot(u_ref[...], bin_ref[direction])

    def recur(direction, slot):
        a_re = jnp.broadcast_to(are_ref[direction], (batch, ns))
        a_im = jnp.broadcast_to(aim_ref[direction], (batch, ns))
        h_re = carry[direction, :, 0:ns]
        h_im = carry[direction, :, ns:2 * ns]
        for s in range(steps):
            t = (steps - 1 - s) if direction == 1 else s
            rows = slice(t * batch, (t + 1) * batch)
            n_re = a_re * h_re - a_im * h_im + bufs[direction, slot, rows, 0:ns]
            n_im = a_re * h_im + a_im * h_re + bufs[direction, slot, rows, ns:2 * ns]
            bufs[direction, slot, rows, 0:ns] = n_re
            bufs[direction, slot, rows, ns:2 * ns] = n_im
            h_re, h_im = n_re, n_im
        carry[direction, :, 0:ns] = h_re
        carry[direction, :, ns:2 * ns] = h_im

    def project_out(direction, slot, position):
        half = position if direction == 0 else 1 - position
        y = _dot(bufs[direction, slot].astype(BF16), cout_ref[direction])
        y_refs[direction][half * tr:(half + 1) * tr, :] = y.astype(BF16)

    @pl.when(m == 0)
    def _():
        carry[...] = jnp.zeros_like(carry)
        for direction in range(2):
            project_in(u_first[direction], direction, 0)

    for direction in range(2):
        project_in(u_odd[direction], direction, 1)
        recur(direction, 0)
    for direction in range(2):
        project_out(direction, 0, 0)
        recur(direction, 1)
    for direction in range(2):
        project_in(u_next[direction], direction, 0)
        project_out(direction, 1, 1)


def _s5_scan(u, b_in, c_out, a_re, a_im, *, batch, n_ctx_tiles):
    rows, d_s5 = u.shape
    nb = b_in.shape[1]
    ns = a_re.shape[-1]
    tr = MIX_STEPS * batch
    nt = rows // tr
    assert nt % 2 == 0 and n_ctx_tiles % 2 == 0

    def rev(p):
        return jnp.where(p < n_ctx_tiles, n_ctx_tiles - 1 - p, nt - 1 + n_ctx_tiles - p)

    nxt = lambda m: jnp.minimum(2 * m + 2, nt - 1)
    chunk = lambda index: pl.BlockSpec((tr, LANES), index)
    param = lambda shape: pl.BlockSpec((2, None) + shape, lambda j, m: (0, j, 0, 0))
    return pl.pallas_call(
        functools.partial(_s5_kernel, batch=batch, steps=MIX_STEPS),
        grid=(nb, nt // 2),
        in_specs=[chunk(lambda j, m: (0, j)),
                  chunk(lambda j, m: (2 * m + 1, j)),
                  chunk(lambda j, m: (nxt(m), j)),
                  chunk(lambda j, m: (rev(0), j)),
                  chunk(lambda j, m: (rev(2 * m + 1), j)),
                  chunk(lambda j, m: (rev(nxt(m)), j)),
                  param((LANES, 2 * ns)), param((2 * ns, LANES)), param((1, ns)), param((1, ns))],
        out_specs=[pl.BlockSpec((2 * tr, LANES), lambda j, m: (m, j)),
                   pl.BlockSpec((2 * tr, LANES), lambda j, m: (rev(2 * m + 1) // 2, j))],
        out_shape=[jax.ShapeDtypeStruct((rows, d_s5), BF16), jax.ShapeDtypeStruct((rows, d_s5), BF16)],
        scratch_shapes=[pltpu.VMEM((2, 2, tr, 2 * ns), F32), pltpu.VMEM((2, batch, 2 * ns), F32)],
        compiler_params=_params("arbitrary", "arbitrary"),
        name="s5_scan",
    )(u, u, u, u, u, u, b_in, c_out, a_re, a_im)


def _mixout_kernel(h_ref, mod_ref, g_ref, u_ref, yf_ref, yb_ref, ac_ref, ap_ref, an_ref, wgate_ref,
                   sd_ref, wglu_ref, wa_ref, dw_ref, dwb_ref, lng_ref, lnb_ref, wb_ref, wout_ref,
                   o_ref, aext, conv, *, batch, n_ctx_tiles, n_tiles, first_tile):
    i = pl.program_id(0) + first_tile
    tr, d = h_ref.shape
    halo = ap_ref.shape[0]
    taps = dw_ref.shape[0]

    ys = sd_ref[...] * u_ref[...].astype(F32) + yf_ref[...].astype(F32) + yb_ref[...].astype(F32)
    g = jax.nn.gelu(ys)
    ya = _dot((g * jax.nn.sigmoid(_dot(g.astype(BF16), wglu_ref[...]))).astype(BF16), wa_ref[...])

    nl = _norm_modulate(h_ref[...], g_ref[...], mod_ref, batch, 0, 1).astype(BF16)
    gate_a = jax.nn.sigmoid(_dot(nl, wgate_ref[:, 0:d]))
    gate_b = jax.nn.sigmoid(_dot(nl, wgate_ref[:, d:2 * d]))

    first = jnp.logical_or(i == 0, i == n_ctx_tiles)
    last = jnp.logical_or(i == n_ctx_tiles - 1, i == n_tiles - 1)
    aext[0:halo] = jnp.where(first, 0.0, ap_ref[...].astype(F32))
    aext[halo:halo + tr] = ac_ref[...].astype(F32)
    aext[halo + tr:2 * halo + tr] = jnp.where(last, 0.0, an_ref[...].astype(F32))
    chunk = 2 * batch

    sub = dw_ref.shape[1]
    width = dw_ref.shape[2]
    for c in range(tr // chunk):
        r0 = c * chunk
        acc = jnp.broadcast_to(dwb_ref[...], (chunk, width)).reshape(chunk // sub, sub, width)
        for k in range(taps):
            off = r0 + halo + (k - taps // 2) * batch
            acc = acc + dw_ref[k][None] * aext[off:off + chunk, :].reshape(chunk // sub, sub, width)
        conv[r0:r0 + chunk, :] = acc.reshape(chunk, width)
    a = conv[...]
    mu = jnp.mean(a, axis=-1, keepdims=True)
    ac = a - mu
    var = jnp.mean(ac * ac, axis=-1, keepdims=True)
    ln = (ac * lax.rsqrt(var + EPS)) * lng_ref[...] + lnb_ref[...]
    yb = _dot((ln * jax.nn.sigmoid(ln)).astype(BF16), wb_ref[...])

    merged = gate_a * ya + gate_b * yb
    y = _dot(merged.astype(BF16), wout_ref[...])
    o_ref[...] = _gated_residual(h_ref[...], y, mod_ref, batch, 2)


def _mixout(h, mod, gain, u, yf, yb, a, w_gate, s5_d, w_glu, w_a, conv_dw, conv_dw_b, ln_g, ln_b, w_b, w_out,
            *, batch, n_ctx_tiles, first_tile):
    rows, d = h.shape
    d_s5 = u.shape[1]
    d_conv = a.shape[1]
    tr = MIX_STEPS * batch
    halo = CONV_HALO_STEPS * batch
    ratio = tr // halo
    nt = rows // tr
    n_halo_blocks = rows // halo
    taps = conv_dw.shape[0]
    assert taps // 2 <= CONV_HALO_STEPS
    sublanes = 8
    dw_rows = jnp.broadcast_to(conv_dw[:, None, :], (taps, sublanes, d_conv))
    kind = lambda i: (jnp.where(i + first_tile >= n_ctx_tiles, 1, 0), 0, 0)
    row_tile = lambda w: pl.BlockSpec((tr, w), lambda i: (i + first_tile, 0))
    const = lambda s: pl.BlockSpec(s, lambda i: (0,) * len(s))
    return pl.pallas_call(
        functools.partial(_mixout_kernel, batch=batch, n_ctx_tiles=n_ctx_tiles, n_tiles=nt, first_tile=first_tile),
        grid=(nt - first_tile,),
        in_specs=[row_tile(d),
                  pl.BlockSpec((None, batch, N_MOD * d), kind),
                  const((1, d)),
                  row_tile(d_s5), row_tile(d_s5), row_tile(d_s5),
                  row_tile(d_conv),
                  pl.BlockSpec((halo, d_conv), lambda i: (jnp.maximum((i + first_tile) * ratio - 1, 0), 0)),
                  pl.BlockSpec((halo, d_conv),
                               lambda i: (jnp.minimum((i + first_tile + 1) * ratio, n_halo_blocks - 1), 0)),
                  const((d, 2 * d)),
                  const((1, d_s5)), const((d_s5, d_s5)), const((d_s5, d)),
                  const((taps, sublanes, d_conv)), const((1, d_conv)), const((1, d_conv)), const((1, d_conv)),
                  const((d_conv, d)), const((d, d))],
        out_specs=pl.BlockSpec((tr, d), lambda i: (i, 0)),
        out_shape=jax.ShapeDtypeStruct((rows - first_tile * tr, d), F32),
        scratch_shapes=[pltpu.VMEM((tr + 2 * halo, d_conv), F32), pltpu.VMEM((tr, d_conv), F32)],
        compiler_params=_params("arbitrary"),
        name="mix_out",
    )(h, mod, gain, u, yf, yb, a, a, a, w_gate, s5_d, w_glu, w_a, dw_rows, conv_dw_b, ln_g, ln_b, w_b, w_out)


def _ffn_gate_kernel(h_ref, mod_ref, g_ref, w_ref, o_ref, *, batch):
    nl = _norm_modulate(h_ref[...], g_ref[...], mod_ref, batch, 3, 4).astype(BF16)
    d_ff = o_ref.shape[1]
    for c in range(d_ff // FFN_BLOCK):
        cols = slice(c * FFN_BLOCK, (c + 1) * FFN_BLOCK)
        o_ref[:, cols] = _dot(nl, w_ref[:, cols]).astype(BF16)


def _ffn_gate(h, mod, gain, w_up, *, batch, n_ctx_tiles, d_ff):
    rows, d = h.shape
    tr = GRID_W * batch
    kind = lambda i: (jnp.where(i >= n_ctx_tiles, 1, 0), 0, 0)
    return pl.pallas_call(
        functools.partial(_ffn_gate_kernel, batch=batch),
        grid=(rows // tr,),
        in_specs=[pl.BlockSpec((tr, d), lambda i: (i, 0)),
                  pl.BlockSpec((None, batch, N_MOD * d), kind),
                  pl.BlockSpec((1, d), lambda i: (0, 0)),
                  pl.BlockSpec((d, d_ff), lambda i: (0, 0))],
        out_specs=pl.BlockSpec((tr, d_ff), lambda i: (i, 0)),
        out_shape=jax.ShapeDtypeStruct((rows, d_ff), BF16),
        compiler_params=_params("arbitrary"),
        name="ffn_gate",
    )(h, mod, gain, w_up)


def _ffn_kernel(h_ref, mod_ref, g_ref, gp_ref, gc_ref, gn_ref, wv_ref, wd_ref, dw_ref, db_ref, fg_ref,
                o_ref, nl_ref, act_ref, *, batch, n_ctx_tiles, n_tiles, final_norm):
    i = pl.program_id(0)
    f = pl.program_id(1)
    nf = pl.num_programs(1)
    tr, d = h_ref.shape
    fw = gc_ref.shape[1]
    ksz = 3

    @pl.when(f == 0)
    def _():
        nl_ref[...] = _norm_modulate(h_ref[...], g_ref[...], mod_ref, batch, 3, 4).astype(BF16)
        o_ref[...] = jnp.zeros_like(o_ref)

    is_ctx = i < n_ctx_tiles
    up_ok = jnp.logical_and(jnp.logical_not(is_ctx), i >= n_ctx_tiles + 1)
    down_ok = jnp.logical_and(jnp.logical_not(is_ctx), i <= n_tiles - 2)
    ctx_lo = jnp.logical_and(is_ctx, i >= 1)
    ctx_hi = jnp.logical_and(is_ctx, i <= n_ctx_tiles - 2)
    row_ok = (up_ok, None, down_ok)
    srcs = (gp_ref, gc_ref, gn_ref)
    group = FFN_ROW_BLOCK // batch
    n_blocks = tr // FFN_ROW_BLOCK

    def conv_block(rb):
        r0 = rb * FFN_ROW_BLOCK
        for c in range(fw // LANES):
            cols = slice(c * LANES, (c + 1) * LANES)
            w = []
            for s in range(ksz):
                for k in range(ksz):
                    wk = dw_ref[s * ksz + k:s * ksz + k + 1, cols]
                    if row_ok[s] is not None:
                        wk = jnp.where(row_ok[s], wk, 0.0)
                    w.append(jnp.broadcast_to(wk.astype(BF16), (batch, LANES)))
            bias = jnp.broadcast_to(db_ref[:, cols].astype(BF16), (batch, LANES))

            def step_rows(s, rj):
                zero = jnp.zeros((batch, LANES), BF16)
                if rj < 0:
                    return jnp.where(ctx_lo, gp_ref[tr - batch:tr, cols], zero) if s == 1 else None
                if rj >= tr:
                    return jnp.where(ctx_hi, gn_ref[0:batch, cols], zero) if s == 1 else None
                return srcs[s][rj:rj + batch, cols]

            window = [[step_rows(s, r0 + j * batch) for j in (-1, 0)] for s in range(ksz)]
            for j in range(group):
                rj = r0 + j * batch
                acc = bias
                for s in range(ksz):
                    window[s].append(step_rows(s, rj + batch))
                    for k in range(ksz):
                        if window[s][k] is not None:
                            acc = acc + w[s * ksz + k] * window[s][k]
                    window[s].pop(0)
                act_ref[rj:rj + batch, cols] = acc * jax.nn.sigmoid(acc)

    def project_block(rb):
        rows = slice(rb * FFN_ROW_BLOCK, (rb + 1) * FFN_ROW_BLOCK)
        nl = nl_ref[rows, :]
        for c0 in range(0, fw, MXU_N):
            cols = slice(c0, min(c0 + MXU_N, fw))
            act_ref[rows, cols] = (act_ref[rows, cols].astype(F32) * _dot(nl, wv_ref[:, cols])).astype(BF16)
        o_ref[rows, :] += _dot(act_ref[rows, :], wd_ref[...])

    conv_block(0)
    for rb in range(n_blocks - 1):
        conv_block(rb + 1)
        project_block(rb)
    project_block(n_blocks - 1)

    @pl.when(f == nf - 1)
    def _():
        out = _gated_residual(h_ref[...], o_ref[...], mod_ref, batch, 5)
        if final_norm:
            ms = jnp.mean(out * out, axis=-1, keepdims=True)
            out = (out * lax.rsqrt(ms + EPS)) * fg_ref[...]
        o_ref[...] = out


def _ffn_block(d_ff):
    half = d_ff // 2
    return half if half % LANES == 0 else d_ff


def _ffn(h, mod, gain, gate, w_up, w_down, dw, dw_b, final_gain, *, batch, n_ctx_tiles, d_ff, final_norm):
    rows, d = h.shape
    tr = GRID_W * batch
    nt = rows // tr
    fw = _ffn_block(d_ff)
    nf = d_ff // fw
    kind = lambda i, f: (jnp.where(i >= n_ctx_tiles, 1, 0), 0, 0)
    return pl.pallas_call(
        functools.partial(_ffn_kernel, batch=batch, n_ctx_tiles=n_ctx_tiles, n_tiles=nt, final_norm=final_norm),
        grid=(nt, nf),
        in_specs=[pl.BlockSpec((tr, d), lambda i, f: (i, 0)),
                  pl.BlockSpec((None, batch, N_MOD * d), kind),
                  pl.BlockSpec((1, d), lambda i, f: (0, 0)),
                  pl.BlockSpec((tr, fw), lambda i, f: (jnp.maximum(i - 1, 0), f)),
                  pl.BlockSpec((tr, fw), lambda i, f: (i, f)),
                  pl.BlockSpec((tr, fw), lambda i, f: (jnp.minimum(i + 1, nt - 1), f)),
                  pl.BlockSpec((d, fw), lambda i, f: (0, nf + f)),
                  pl.BlockSpec((fw, d), lambda i, f: (f, 0)),
                  pl.BlockSpec((9, fw), lambda i, f: (0, f)),
                  pl.BlockSpec((1, fw), lambda i, f: (0, f)),
                  pl.BlockSpec((1, d), lambda i, f: (0, 0))],
        out_specs=pl.BlockSpec((tr, d), lambda i, f: (i, 0)),
        out_shape=jax.ShapeDtypeStruct((rows, d), F32),
        scratch_shapes=[pltpu.VMEM((tr, d), BF16), pltpu.VMEM((tr, fw), BF16)],
        compiler_params=pltpu.CompilerParams(dimension_semantics=("arbitrary", "arbitrary"),
                                             vmem_limit_bytes=FFN_VMEM_LIMIT),
        name="ffn_main",
    )(h, mod, gain, gate, gate, gate, w_up, w_down, dw, dw_b, final_gain)


def kernel(x, c, ctx, c_ctx, ada_w, ada_b, norm1_g, w_in, s5_lam_re, s5_lam_im, s5_log_dt, s5_b_re, s5_b_im, s5_c_re, s5_c_im, s5_d, w_glu, w_a, conv_dw, conv_dw_b, conv_ln_g, conv_ln_b, w_b, w_out, norm2_g, ffn_w_up, ffn_dw, ffn_dw_b, ffn_w_down, final_g):
    batch, seq, d = x.shape
    ctx_len = ctx.shape[1]
    depth = ada_w.shape[0]
    d_s5 = w_glu.shape[1]
    d_conv = conv_dw.shape[2]
    d_ff = ffn_dw.shape[-1]
    assert batch % 16 == 0 and seq % GRID_W == 0 and ctx_len % GRID_W == 0
    assert d_s5 % LANES == 0 and d_ff % FFN_BLOCK == 0
    mix_ctx_tiles = ctx_len // MIX_STEPS
    ffn_ctx_tiles = ctx_len // GRID_W

    h = jnp.concatenate([jnp.swapaxes(ctx, 0, 1), jnp.swapaxes(x, 0, 1)], axis=0).reshape(-1, d)

    cc = jnp.concatenate([jnp.broadcast_to(c_ctx[None], (batch, d)), c], axis=0)
    mods = _modulation(cc, ada_w, ada_b).reshape(depth, 2, batch, N_MOD * d)

    ar, ai, bbr, bbi = _discretise(s5_lam_re, s5_lam_im, s5_log_dt, s5_b_re, s5_b_im)
    b_in, c_out, a_re, a_im = _s5_operands(ar, ai, bbr, bbi, s5_c_re, s5_c_im)

    row = lambda v: v.reshape(1, -1)
    for i in range(depth):
        mod = mods[i]
        off_gate = d_s5 + 2 * d_conv
        u, a = _inproj(h, mod, row(norm1_g[i]), w_in[i, :, :off_gate].astype(BF16), batch=batch,
                       n_ctx_tiles=ffn_ctx_tiles, d_s5=d_s5, d_conv=d_conv)
        yf, yb = _s5_scan(u, b_in[i], c_out[i], a_re[i], a_im[i], batch=batch, n_ctx_tiles=mix_ctx_tiles)
        last = i == depth - 1
        h = _mixout(h, mod, row(norm1_g[i]), u, yf, yb, a, w_in[i, :, off_gate:].astype(BF16),
                    row(s5_d[i]), w_glu[i].astype(BF16), w_a[i].astype(BF16),
                    conv_dw[i], row(conv_dw_b[i]), row(conv_ln_g[i]), row(conv_ln_b[i]),
                    w_b[i].astype(BF16), w_out[i].astype(BF16), batch=batch, n_ctx_tiles=mix_ctx_tiles,
                    first_tile=mix_ctx_tiles if last else 0)
        ctx_tiles = 0 if last else ffn_ctx_tiles
        w_up = ffn_w_up[i].astype(BF16)
        gate = _ffn_gate(h, mod, row(norm2_g[i]), w_up, batch=batch, n_ctx_tiles=ctx_tiles, d_ff=d_ff)
        h = _ffn(h, mod, row(norm2_g[i]), gate, w_up, ffn_w_down[i].astype(BF16),
                 ffn_dw[i].reshape(9, d_ff), row(ffn_dw_b[i]), row(final_g), batch=batch, n_ctx_tiles=ctx_tiles,
                 d_ff=d_ff, final_norm=last)

    return jnp.swapaxes(h.reshape(seq, batch, d), 0, 1)
```

```python
import functools

import jax
import jax.numpy as jnp
from jax import lax
from jax.experimental import pallas as pl
from jax.experimental.pallas import tpu as pltpu

GRID_W = 64
S5_GROUP = 16
S5_STATE = 64
N_MOD = 6
EPS = 1e-6
LANES = 128
S5_BLOCK_GROUPS = LANES // S5_GROUP
MIX_STEPS = 32
CONV_HALO_STEPS = 16
FFN_BLOCK = 256
FFN_ROW_BLOCK = 512
MXU_N = 256
VMEM_LIMIT = 56 * 1024 * 1024
FFN_VMEM_LIMIT = 60 * 1024 * 1024

BF16 = jnp.bfloat16
F32 = jnp.float32


def _dot(a, b):
    return jnp.dot(a, b, preferred_element_type=F32)


def _split_bf16(a):
    hi = a.astype(BF16)
    lo = (a - hi.astype(F32)).astype(BF16)
    return hi, lo


def _dot_3pass(a, b):
    a_hi, a_lo = _split_bf16(a)
    b_hi, b_lo = _split_bf16(b)
    return _dot(a_hi, b_hi) + (_dot(a_lo, b_hi) + _dot(a_hi, b_lo))


def _params(*sem):
    return pltpu.CompilerParams(dimension_semantics=sem, vmem_limit_bytes=VMEM_LIMIT)


def _norm_modulate(x, gain, mod_ref, batch, k_shift, k_scale):
    rows, d = x.shape
    ms = jnp.mean(x * x, axis=-1, keepdims=True)
    n = (x * lax.rsqrt(ms + EPS)) * gain
    shift = mod_ref[:, k_shift * d:(k_shift + 1) * d]
    scale = mod_ref[:, k_scale * d:(k_scale + 1) * d]
    n3 = n.reshape(rows // batch, batch, d)
    out = n3 * (1.0 + scale)[None] + shift[None]
    return out.reshape(rows, d)


def _gated_residual(x, y, mod_ref, batch, k_gate):
    rows, d = x.shape
    gate = mod_ref[:, k_gate * d:(k_gate + 1) * d]
    out = x.reshape(rows // batch, batch, d) + gate[None] * y.reshape(rows // batch, batch, d)
    return out.reshape(rows, d)


def _mod_kernel(c_ref, w_ref, b_ref, o_ref):
    c = c_ref[...]
    s = c * jax.nn.sigmoid(c)
    o_ref[...] = _dot_3pass(s, w_ref[...]) + b_ref[...]


def _modulation(cc, ada_w, ada_b):
    depth, d, n = ada_w.shape
    rows = cc.shape[0]
    tn = d
    return pl.pallas_call(
        _mod_kernel,
        grid=(depth, n // tn),
        in_specs=[pl.BlockSpec((rows, d), lambda l, j: (0, 0)),
                  pl.BlockSpec((None, d, tn), lambda l, j: (l, 0, j)),
                  pl.BlockSpec((None, 1, tn), lambda l, j: (l, 0, j))],
        out_specs=pl.BlockSpec((None, rows, tn), lambda l, j: (l, 0, j)),
        out_shape=jax.ShapeDtypeStruct((depth, rows, n), F32),
        compiler_params=_params("arbitrary", "arbitrary"),
        name="adaln_mod",
    )(cc, ada_w, ada_b.reshape(depth, 1, n))


def _disc_kernel(lr_ref, li_ref, ldt_ref, br_ref, bi_ref, ar_ref, ai_ref, bbr_ref, bbi_ref):
    dt = jnp.exp(ldt_ref[...])
    lr = lr_ref[...]
    li = li_ref[...]
    mag = jnp.exp(dt * lr)
    ab_re = mag * jnp.cos(dt * li)
    ab_im = mag * jnp.sin(dt * li)
    den = lr * lr + li * li
    nr = ab_re - 1.0
    k_re = (nr * lr + ab_im * li) / den
    k_im = (ab_im * lr - nr * li) / den
    br = br_ref[...]
    bi = bi_ref[...]
    ar_ref[...] = ab_re
    ai_ref[...] = ab_im
    bbr_ref[...] = k_re * br - k_im * bi
    bbi_ref[...] = k_re * bi + k_im * br


def _discretise(lam_re, lam_im, log_dt, b_re, b_im):
    lead = lam_re.shape[:-1]
    p = lam_re.shape[-1]
    h = b_re.shape[-1]
    r = 1
    for s in lead:
        r *= s
    lr = lam_re.reshape(r, 1, p)
    li = lam_im.reshape(r, 1, p)
    ldt = jnp.broadcast_to(log_dt.reshape(r, 1, 1), (r, 1, p))
    br = jnp.swapaxes(b_re, -1, -2).reshape(r, h, p)
    bi = jnp.swapaxes(b_im, -1, -2).reshape(r, h, p)
    full = lambda shape: pl.BlockSpec(shape, lambda i: (0,) * len(shape))
    ar, ai, bbr, bbi = pl.pallas_call(
        _disc_kernel,
        grid=(1,),
        in_specs=[full((r, 1, p)), full((r, 1, p)), full((r, 1, p)), full((r, h, p)), full((r, h, p))],
        out_specs=[full((r, 1, p)), full((r, 1, p)), full((r, h, p)), full((r, h, p))],
        out_shape=[jax.ShapeDtypeStruct((r, 1, p), F32), jax.ShapeDtypeStruct((r, 1, p), F32),
                   jax.ShapeDtypeStruct((r, h, p), F32), jax.ShapeDtypeStruct((r, h, p), F32)],
        compiler_params=_params("arbitrary"),
        name="s5_discretise",
    )(lr, li, ldt, br, bi)
    return (ar.reshape(lead + (p,)), ai.reshape(lead + (p,)),
            bbr.reshape(lead + (h, p)), bbi.reshape(lead + (h, p)))


def _block_diag(m):
    g = m.shape[-3]
    eye = jnp.eye(g, dtype=m.dtype)
    out = m[..., :, :, None, :] * eye[:, None, :, None]
    return out.reshape(m.shape[:-3] + (g * m.shape[-2], g * m.shape[-1]))


def _s5_operands(ar, ai, bbr, bbi, c_re, c_im):
    lead = ar.shape[:-2]
    g, p = ar.shape[-2:]
    h = bbr.shape[-2]
    nb = g // S5_BLOCK_GROUPS
    bg = S5_BLOCK_GROUPS
    blk = lambda m: m.astype(BF16).reshape(lead + (nb, bg) + m.shape[-2:])
    b_in = jnp.concatenate([_block_diag(blk(bbr)), _block_diag(blk(bbi))], axis=-1)
    ct = lambda m: jnp.swapaxes(m, -1, -2)
    c_out = jnp.concatenate([_block_diag(blk(ct(c_re))), _block_diag(blk(ct(-c_im)))], axis=-2)
    a_re = ar.reshape(lead + (nb, 1, bg * p))
    a_im = ai.reshape(lead + (nb, 1, bg * p))
    return b_in, c_out, a_re, a_im


def _inproj_kernel(h_ref, mod_ref, g_ref, w_ref, u_ref, a_ref, *, batch, d_s5, d_conv):
    nl = _norm_modulate(h_ref[...], g_ref[...], mod_ref, batch, 0, 1).astype(BF16)
    u_ref[...] = _dot(nl, w_ref[:, :d_s5]).astype(BF16)
    off = d_s5
    v1 = _dot(nl, w_ref[:, off:off + d_conv])
    v2 = _dot(nl, w_ref[:, off + d_conv:off + 2 * d_conv])
    a_ref[...] = (v1 * jax.nn.sigmoid(v2)).astype(BF16)


def _inproj(h, mod, gain, w_in, *, batch, n_ctx_tiles, d_s5, d_conv):
    rows, d = h.shape
    tr = GRID_W * batch
    d_in = w_in.shape[1]
    kind = lambda i: (jnp.where(i >= n_ctx_tiles, 1, 0), 0, 0)
    return pl.pallas_call(
        functools.partial(_inproj_kernel, batch=batch, d_s5=d_s5, d_conv=d_conv),
        grid=(rows // tr,),
        in_specs=[pl.BlockSpec((tr, d), lambda i: (i, 0)),
                  pl.BlockSpec((None, batch, N_MOD * d), kind),
                  pl.BlockSpec((1, d), lambda i: (0, 0)),
                  pl.BlockSpec((d, d_in), lambda i: (0, 0))],
        out_specs=[pl.BlockSpec((tr, d_s5), lambda i: (i, 0)),
                   pl.BlockSpec((tr, d_conv), lambda i: (i, 0))],
        out_shape=[jax.ShapeDtypeStruct((rows, d_s5), BF16),
                   jax.ShapeDtypeStruct((rows, d_conv), BF16)],
        compiler_params=_params("arbitrary"),
        name="mix_inproj",
    )(h, mod, gain, w_in)


def _s5_kernel(uf0_ref, uf1_ref, uf2_ref, ub0_ref, ub1_ref, ub2_ref, bin_ref, cout_ref, are_ref, aim_ref,
               yf_ref, yb_ref, bufs, carry, *, batch, steps):
    m = pl.program_id(1)
    ns = are_ref.shape[-1]
    tr = steps * batch
    u_first = (uf0_ref, ub0_ref)
    u_odd = (uf1_ref, ub1_ref)
    u_next = (uf2_ref, ub2_ref)
    y_refs = (yf_ref, yb_ref)

    def project_in(u_ref, direction, slot):
        bufs[direction, slot] = _dot(u_ref[...], bin_ref[direction])

    def recur(direction, slot):
        a_re = jnp.broadcast_to(are_ref[direction], (batch, ns))
        a_im = jnp.broadcast_to(aim_ref[direction], (batch, ns))
        h_re = carry[direction, :, 0:ns]
        h_im = carry[direction, :, ns:2 * ns]
        for s in range(steps):
            t = (steps - 1 - s) if direction == 1 else s
            rows = slice(t * batch, (t + 1) * batch)
            n_re = a_re * h_re - a_im * h_im + bufs[direction, slot, rows, 0:ns]
            n_im = a_re * h_im + a_im * h_re + bufs[direction, slot, rows, ns:2 * ns]
            bufs[direction, slot, rows, 0:ns] = n_re
            bufs[direction, slot, rows, ns:2 * ns] = n_im
            h_re, h_im = n_re, n_im
        carry[direction, :, 0:ns] = h_re
        carry[direction, :, ns:2 * ns] = h_im

    def project_out(direction, slot, position):
        half = position if direction == 0 else 1 - position
        y = _dot(bufs[direction, slot].astype(BF16), cout_ref[direction])
        y_refs[direction][half * tr:(half + 1) * tr, :] = y.astype(BF16)

    @pl.when(m == 0)
    def _():
        carry[...] = jnp.zeros_like(carry)
        for direction in range(2):
            project_in(u_first[direction], direction, 0)

    for direction in range(2):
        project_in(u_odd[direction], direction, 1)
        recur(direction, 0)
    for direction in range(2):
        project_out(direction, 0, 0)
        recur(direction, 1)
    for direction in range(2):
        project_in(u_next[direction], direction, 0)
        project_out(direction, 1, 1)


def _s5_scan(u, b_in, c_out, a_re, a_im, *, batch, n_ctx_tiles):
    rows, d_s5 = u.shape
    nb = b_in.shape[1]
    ns = a_re.shape[-1]
    tr = MIX_STEPS * batch
    nt = rows // tr
    assert nt % 2 == 0 and n_ctx_tiles % 2 == 0

    def rev(p):
        return jnp.where(p < n_ctx_tiles, n_ctx_tiles - 1 - p, nt - 1 + n_ctx_tiles - p)

    nxt = lambda m: jnp.minimum(2 * m + 2, nt - 1)
    chunk = lambda index: pl.BlockSpec((tr, LANES), index)
    param = lambda shape: pl.BlockSpec((2, None) + shape, lambda j, m: (0, j, 0, 0))
    return pl.pallas_call(
        functools.partial(_s5_kernel, batch=batch, steps=MIX_STEPS),
        grid=(nb, nt // 2),
        in_specs=[chunk(lambda j, m: (0, j)),
                  chunk(lambda j, m: (2 * m + 1, j)),
                  chunk(lambda j, m: (nxt(m), j)),
                  chunk(lambda j, m: (rev(0), j)),
                  chunk(lambda j, m: (rev(2 * m + 1), j)),
                  chunk(lambda j, m: (rev(nxt(m)), j)),
                  param((LANES, 2 * ns)), param((2 * ns, LANES)), param((1, ns)), param((1, ns))],
        out_specs=[pl.BlockSpec((2 * tr, LANES), lambda j, m: (m, j)),
                   pl.BlockSpec((2 * tr, LANES), lambda j, m: (rev(2 * m + 1) // 2, j))],
        out_shape=[jax.ShapeDtypeStruct((rows, d_s5), BF16), jax.ShapeDtypeStruct((rows, d_s5), BF16)],
        scratch_shapes=[pltpu.VMEM((2, 2, tr, 2 * ns), F32), pltpu.VMEM((2, batch, 2 * ns), F32)],
        compiler_params=_params("arbitrary", "arbitrary"),
        name="s5_scan",
    )(u, u, u, u, u, u, b_in, c_out, a_re, a_im)


def _mixout_kernel(h_ref, mod_ref, g_ref, u_ref, yf_ref, yb_ref, ac_ref, ap_ref, an_ref, wgate_ref,
                   sd_ref, wglu_ref, wa_ref, dw_ref, dwb_ref, lng_ref, lnb_ref, wb_ref, wout_ref,
                   o_ref, aext, conv, *, batch, n_ctx_tiles, n_tiles, first_tile):
    i = pl.program_id(0) + first_tile
    tr, d = h_ref.shape
    halo = ap_ref.shape[0]
    taps = dw_ref.shape[0]

    ys = sd_ref[...] * u_ref[...].astype(F32) + yf_ref[...].astype(F32) + yb_ref[...].astype(F32)
    g = jax.nn.gelu(ys)
    ya = _dot((g * jax.nn.sigmoid(_dot(g.astype(BF16), wglu_ref[...]))).astype(BF16), wa_ref[...])

    nl = _norm_modulate(h_ref[...], g_ref[...], mod_ref, batch, 0, 1).astype(BF16)
    gate_a = jax.nn.sigmoid(_dot(nl, wgate_ref[:, 0:d]))
    gate_b = jax.nn.sigmoid(_dot(nl, wgate_ref[:, d:2 * d]))

    first = jnp.logical_or(i == 0, i == n_ctx_tiles)
    last = jnp.logical_or(i == n_ctx_tiles - 1, i == n_tiles - 1)
    aext[0:halo] = jnp.where(first, 0.0, ap_ref[...].astype(F32))
    aext[halo:halo + tr] = ac_ref[...].astype(F32)
    aext[halo + tr:2 * halo + tr] = jnp.where(last, 0.0, an_ref[...].astype(F32))
    chunk = 2 * batch

    sub = dw_ref.shape[1]
    width = dw_ref.shape[2]
    for c in range(tr // chunk):
        r0 = c * chunk
        acc = jnp.broadcast_to(dwb_ref[...], (chunk, width)).reshape(chunk // sub, sub, width)
        for k in range(taps):
            off = r0 + halo + (k - taps // 2) * batch
            acc = acc + dw_ref[k][None] * aext[off:off + chunk, :].reshape(chunk // sub, sub, width)
        conv[r0:r0 + chunk, :] = acc.reshape(chunk, width)
    a = conv[...]
    mu = jnp.mean(a, axis=-1, keepdims=True)
    ac = a - mu
    var = jnp.mean(ac * ac, axis=-1, keepdims=True)
    ln = (ac * lax.rsqrt(var + EPS)) * lng_ref[...] + lnb_ref[...]
    yb = _dot((ln * jax.nn.sigmoid(ln)).astype(BF16), wb_ref[...])

    merged = gate_a * ya + gate_b * yb
    y = _dot(merged.astype(BF16), wout_ref[...])
    o_ref[...] = _gated_residual(h_ref[...], y, mod_ref, batch, 2)


def _mixout(h, mod, gain, u, yf, yb, a, w_gate, s5_d, w_glu, w_a, conv_dw, conv_dw_b, ln_g, ln_b, w_b, w_out,
            *, batch, n_ctx_tiles, first_tile):
    rows, d = h.shape
    d_s5 = u.shape[1]
    d_conv = a.shape[1]
    tr = MIX_STEPS * batch
    halo = CONV_HALO_STEPS * batch
    ratio = tr // halo
    nt = rows // tr
    n_halo_blocks = rows // halo
    taps = conv_dw.shape[0]
    assert taps // 2 <= CONV_HALO_STEPS
    sublanes = 8
    dw_rows = jnp.broadcast_to(conv_dw[:, None, :], (taps, sublanes, d_conv))
    kind = lambda i: (jnp.where(i + first_tile >= n_ctx_tiles, 1, 0), 0, 0)
    row_tile = lambda w: pl.BlockSpec((tr, w), lambda i: (i + first_tile, 0))
    const = lambda s: pl.BlockSpec(s, lambda i: (0,) * len(s))
    return pl.pallas_call(
        functools.partial(_mixout_kernel, batch=batch, n_ctx_tiles=n_ctx_tiles, n_tiles=nt, first_tile=first_tile),
        grid=(nt - first_tile,),
        in_specs=[row_tile(d),
                  pl.BlockSpec((None, batch, N_MOD * d), kind),
                  const((1, d)),
                  row_tile(d_s5), row_tile(d_s5), row_tile(d_s5),
                  row_tile(d_conv),
                  pl.BlockSpec((halo, d_conv), lambda i: (jnp.maximum((i + first_tile) * ratio - 1, 0), 0)),
                  pl.BlockSpec((halo, d_conv),
                               lambda i: (jnp.minimum((i + first_tile + 1) * ratio, n_halo_blocks - 1), 0)),
                  const((d, 2 * d)),
                  const((1, d_s5)), const((d_s5, d_s5)), const((d_s5, d)),
                  const((taps, sublanes, d_conv)), const((1, d_conv)), const((1, d_conv)), const((1, d_conv)),
                  const((d_conv, d)), const((d, d))],
        out_specs=pl.BlockSpec((tr, d), lambda i: (i, 0)),
        out_shape=jax.ShapeDtypeStruct((rows - first_tile * tr, d), F32),
        scratch_shapes=[pltpu.VMEM((tr + 2 * halo, d_conv), F32), pltpu.VMEM((tr, d_conv), F32)],
        compiler_params=_params("arbitrary"),
        name="mix_out",
    )(h, mod, gain, u, yf, yb, a, a, a, w_gate, s5_d, w_glu, w_a, dw_rows, conv_dw_b, ln_g, ln_b, w_b, w_out)


def _ffn_gate_kernel(h_ref, mod_ref, g_ref, w_ref, o_ref, *, batch):
    nl = _norm_modulate(h_ref[...], g_ref[...], mod_ref, batch, 3, 4).astype(BF16)
    d_ff = o_ref.shape[1]
    for c in range(d_ff // FFN_BLOCK):
        cols = slice(c * FFN_BLOCK, (c + 1) * FFN_BLOCK)
        o_ref[:, cols] = _dot(nl, w_ref[:, cols]).astype(BF16)


def _ffn_gate(h, mod, gain, w_up, *, batch, n_ctx_tiles, d_ff):
    rows, d = h.shape
    tr = GRID_W * batch
    kind = lambda i: (jnp.where(i >= n_ctx_tiles, 1, 0), 0, 0)
    return pl.pallas_call(
        functools.partial(_ffn_gate_kernel, batch=batch),
        grid=(rows // tr,),
        in_specs=[pl.BlockSpec((tr, d), lambda i: (i, 0)),
                  pl.BlockSpec((None, batch, N_MOD * d), kind),
                  pl.BlockSpec((1, d), lambda i: (0, 0)),
                  pl.BlockSpec((d, d_ff), lambda i: (0, 0))],
        out_specs=pl.BlockSpec((tr, d_ff), lambda i: (i, 0)),
        out_shape=jax.ShapeDtypeStruct((rows, d_ff), BF16),
        compiler_params=_params("arbitrary"),
        name="ffn_gate",
    )(h, mod, gain, w_up)


def _ffn_kernel(h_ref, mod_ref, g_ref, gp_ref, gc_ref, gn_ref, wv_ref, wd_ref, dw_ref, db_ref, fg_ref,
                o_ref, nl_ref, act_ref, *, batch, n_ctx_tiles, n_tiles, final_norm):
    i = pl.program_id(0)
    f = pl.program_id(1)
    nf = pl.num_programs(1)
    tr, d = h_ref.shape
    fw = gc_ref.shape[1]
    ksz = 3

    @pl.when(f == 0)
    def _():
        nl_ref[...] = _norm_modulate(h_ref[...], g_ref[...], mod_ref, batch, 3, 4).astype(BF16)
        o_ref[...] = jnp.zeros_like(o_ref)

    is_ctx = i < n_ctx_tiles
    up_ok = jnp.logical_and(jnp.logical_not(is_ctx), i >= n_ctx_tiles + 1)
    down_ok = jnp.logical_and(jnp.logical_not(is_ctx), i <= n_tiles - 2)
    ctx_lo = jnp.logical_and(is_ctx, i >= 1)
    ctx_hi = jnp.logical_and(is_ctx, i <= n_ctx_tiles - 2)
    row_ok = (up_ok, None, down_ok)
    srcs = (gp_ref, gc_ref, gn_ref)
    group = FFN_ROW_BLOCK // batch
    n_blocks = tr // FFN_ROW_BLOCK

    def conv_block(rb):
        r0 = rb * FFN_ROW_BLOCK
        for c in range(fw // LANES):
            cols = slice(c * LANES, (c + 1) * LANES)
            w = []
            for s in range(ksz):
                for k in range(ksz):
                    wk = dw_ref[s * ksz + k:s * ksz + k + 1, cols]
                    if row_ok[s] is not None:
                        wk = jnp.where(row_ok[s], wk, 0.0)
                    w.append(jnp.broadcast_to(wk.astype(BF16), (batch, LANES)))
            bias = jnp.broadcast_to(db_ref[:, cols].astype(BF16), (batch, LANES))

            def step_rows(s, rj):
                zero = jnp.zeros((batch, LANES), BF16)
                if rj < 0:
                    return jnp.where(ctx_lo, gp_ref[tr - batch:tr, cols], zero) if s == 1 else None
                if rj >= tr:
                    return jnp.where(ctx_hi, gn_ref[0:batch, cols], zero) if s == 1 else None
                return srcs[s][rj:rj + batch, cols]

            window = [[step_rows(s, r0 + j * batch) for j in (-1, 0)] for s in range(ksz)]
            for j in range(group):
                rj = r0 + j * batch
                acc = bias
                for s in range(ksz):
                    window[s].append(step_rows(s, rj + batch))
                    for k in range(ksz):
                        if window[s][k] is not None:
                            acc = acc + w[s * ksz + k] * window[s][k]
                    window[s].pop(0)
                act_ref[rj:rj + batch, cols] = acc * jax.nn.sigmoid(acc)

    def project_block(rb):
        rows = slice(rb * FFN_ROW_BLOCK, (rb + 1) * FFN_ROW_BLOCK)
        nl = nl_ref[rows, :]
        for c0 in range(0, fw, MXU_N):
            cols = slice(c0, min(c0 + MXU_N, fw))
            act_ref[rows, cols] = (act_ref[rows, cols].astype(F32) * _dot(nl, wv_ref[:, cols])).astype(BF16)
        o_ref[rows, :] += _dot(act_ref[rows, :], wd_ref[...])

    conv_block(0)
    for rb in range(n_blocks - 1):
        conv_block(rb + 1)
        project_block(rb)
    project_block(n_blocks - 1)

    @pl.when(f == nf - 1)
    def _():
        out = _gated_residual(h_ref[...], o_ref[...], mod_ref, batch, 5)
        if final_norm:
            ms = jnp.mean(out * out, axis=-1, keepdims=True)
            out = (out * lax.rsqrt(ms + EPS)) * fg_ref[...]
        o_ref[...] = out


def _ffn_block(d_ff):
    half = d_ff // 2
    return half if half % LANES == 0 else d_ff


def _ffn(h, mod, gain, gate, w_up, w_down, dw, dw_b, final_gain, *, batch, n_ctx_tiles, d_ff, final_norm):
    rows, d = h.shape
    tr = GRID_W * batch
    nt = rows // tr
    fw = _ffn_block(d_ff)
    nf = d_ff // fw
    kind = lambda i, f: (jnp.where(i >= n_ctx_tiles, 1, 0), 0, 0)
    return pl.pallas_call(
        functools.partial(_ffn_kernel, batch=batch, n_ctx_tiles=n_ctx_tiles, n_tiles=nt, final_norm=final_norm),
        grid=(nt, nf),
        in_specs=[pl.BlockSpec((tr, d), lambda i, f: (i, 0)),
                  pl.BlockSpec((None, batch, N_MOD * d), kind),
                  pl.BlockSpec((1, d), lambda i, f: (0, 0)),
                  pl.BlockSpec((tr, fw), lambda i, f: (jnp.maximum(i - 1, 0), f)),
                  pl.BlockSpec((tr, fw), lambda i, f: (i, f)),
                  pl.BlockSpec((tr, fw), lambda i, f: (jnp.minimum(i + 1, nt - 1), f)),
                  pl.BlockSpec((d, fw), lambda i, f: (0, nf + f)),
                  pl.BlockSpec((fw, d), lambda i, f: (f, 0)),
                  pl.BlockSpec((9, fw), lambda i, f: (0, f)),
                  pl.BlockSpec((1, fw), lambda i, f: (0, f)),
                  pl.BlockSpec((1, d), lambda i, f: (0, 0))],
        out_specs=pl.BlockSpec((tr, d), lambda i, f: (i, 0)),
        out_shape=jax.ShapeDtypeStruct((rows, d), F32),
        scratch_shapes=[pltpu.VMEM((tr, d), BF16), pltpu.VMEM((tr, fw), BF16)],
        compiler_params=pltpu.CompilerParams(dimension_semantics=("arbitrary", "arbitrary"),
                                             vmem_limit_bytes=FFN_VMEM_LIMIT),
        name="ffn_main",
    )(h, mod, gain, gate, gate, gate, w_up, w_down, dw, dw_b, final_gain)


def kernel(x, c, ctx, c_ctx, ada_w, ada_b, norm1_g, w_in, s5_lam_re, s5_lam_im, s5_log_dt, s5_b_re, s5_b_im, s5_c_re, s5_c_im, s5_d, w_glu, w_a, conv_dw, conv_dw_b, conv_ln_g, conv_ln_b, w_b, w_out, norm2_g, ffn_w_up, ffn_dw, ffn_dw_b, ffn_w_down, final_g):
    batch, seq, d = x.shape
    ctx_len = ctx.shape[1]
    depth = ada_w.shape[0]
    d_s5 = w_glu.shape[1]
    d_conv = conv_dw.shape[2]
    d_ff = ffn_dw.shape[-1]
    assert batch % 16 == 0 and seq % GRID_W == 0 and ctx_len % GRID_W == 0
    assert d_s5 % LANES == 0 and d_ff % FFN_BLOCK == 0
    mix_ctx_tiles = ctx_len // MIX_STEPS
    ffn_ctx_tiles = ctx_len // GRID_W

    h = jnp.concatenate([jnp.swapaxes(ctx, 0, 1), jnp.swapaxes(x, 0, 1)], axis=0).reshape(-1, d)

    cc = jnp.concatenate([jnp.broadcast_to(c_ctx[None], (batch, d)), c], axis=0)
    mods = _modulation(cc, ada_w, ada_b).reshape(depth, 2, batch, N_MOD * d)

    ar, ai, bbr, bbi = _discretise(s5_lam_re, s5_lam_im, s5_log_dt, s5_b_re, s5_b_im)
    b_in, c_out, a_re, a_im = _s5_operands(ar, ai, bbr, bbi, s5_c_re, s5_c_im)

    row = lambda v: v.reshape(1, -1)
    for i in range(depth):
        mod = mods[i]
        off_gate = d_s5 + 2 * d_conv
        u, a = _inproj(h, mod, row(norm1_g[i]), w_in[i, :, :off_gate].astype(BF16), batch=batch,
                       n_ctx_tiles=ffn_ctx_tiles, d_s5=d_s5, d_conv=d_conv)
        yf, yb = _s5_scan(u, b_in[i], c_out[i], a_re[i], a_im[i], batch=batch, n_ctx_tiles=mix_ctx_tiles)
        last = i == depth - 1
        h = _mixout(h, mod, row(norm1_g[i]), u, yf, yb, a, w_in[i, :, off_gate:].astype(BF16),
                    row(s5_d[i]), w_glu[i].astype(BF16), w_a[i].astype(BF16),
                    conv_dw[i], row(conv_dw_b[i]), row(conv_ln_g[i]), row(conv_ln_b[i]),
                    w_b[i].astype(BF16), w_out[i].astype(BF16), batch=batch, n_ctx_tiles=mix_ctx_tiles,
                    first_tile=mix_ctx_tiles if last else 0)
        ctx_tiles = 0 if last else ffn_ctx_tiles
        w_up = ffn_w_up[i].astype(BF16)
        gate = _ffn_gate(h, mod, row(norm2_g[i]), w_up, batch=batch, n_ctx_tiles=ctx_tiles, d_ff=d_ff)
        h = _ffn(h, mod, row(norm2_g[i]), gate, w_up, ffn_w_down[i].astype(BF16),
                 ffn_dw[i].reshape(9, d_ff), row(ffn_dw_b[i]), row(final_g), batch=batch, n_ctx_tiles=ctx_tiles,
                 d_ff=d_ff, final_norm=last)

    return jnp.swapaxes(h.reshape(seq, batch, d), 0, 1)
```

```python
import functools

import jax
import jax.numpy as jnp
from jax import lax
from jax.experimental import pallas as pl
from jax.experimental.pallas import tpu as pltpu

GRID_W = 64
S5_GROUP = 16
S5_STATE = 64
N_MOD = 6
EPS = 1e-6
LANES = 128
S5_BLOCK_GROUPS = LANES // S5_GROUP
MIX_STEPS = 32
S5_STEPS = 64
CONV_HALO_STEPS = 16
FFN_BLOCK = 256
FFN_ROW_BLOCK = 512
MXU_N = 256
VMEM_LIMIT = 56 * 1024 * 1024
FFN_VMEM_LIMIT = 60 * 1024 * 1024

BF16 = jnp.bfloat16
F32 = jnp.float32


def _dot(a, b):
    return jnp.dot(a, b, preferred_element_type=F32)


def _split_bf16(a):
    hi = a.astype(BF16)
    lo = (a - hi.astype(F32)).astype(BF16)
    return hi, lo


def _dot_3pass(a, b):
    a_hi, a_lo = _split_bf16(a)
    b_hi, b_lo = _split_bf16(b)
    return _dot(a_hi, b_hi) + (_dot(a_lo, b_hi) + _dot(a_hi, b_lo))


def _params(*sem):
    return pltpu.CompilerParams(dimension_semantics=sem, vmem_limit_bytes=VMEM_LIMIT)


def _norm_modulate(x, gain, mod_ref, batch, k_shift, k_scale):
    rows, d = x.shape
    ms = jnp.mean(x * x, axis=-1, keepdims=True)
    n = (x * lax.rsqrt(ms + EPS)) * gain
    shift = mod_ref[:, k_shift * d:(k_shift + 1) * d]
    scale = mod_ref[:, k_scale * d:(k_scale + 1) * d]
    n3 = n.reshape(rows // batch, batch, d)
    out = n3 * (1.0 + scale)[None] + shift[None]
    return out.reshape(rows, d)


def _gated_residual(x, y, mod_ref, batch, k_gate):
    rows, d = x.shape
    gate = mod_ref[:, k_gate * d:(k_gate + 1) * d]
    out = x.reshape(rows // batch, batch, d) + gate[None] * y.reshape(rows // batch, batch, d)
    return out.reshape(rows, d)


def _mod_kernel(c_ref, w_ref, b_ref, o_ref):
    c = c_ref[...]
    s = c * jax.nn.sigmoid(c)
    o_ref[...] = _dot_3pass(s, w_ref[...]) + b_ref[...]


def _modulation(cc, ada_w, ada_b):
    depth, d, n = ada_w.shape
    rows = cc.shape[0]
    tn = d
    return pl.pallas_call(
        _mod_kernel,
        grid=(depth, n // tn),
        in_specs=[pl.BlockSpec((rows, d), lambda l, j: (0, 0)),
                  pl.BlockSpec((None, d, tn), lambda l, j: (l, 0, j)),
                  pl.BlockSpec((None, 1, tn), lambda l, j: (l, 0, j))],
        out_specs=pl.BlockSpec((None, rows, tn), lambda l, j: (l, 0, j)),
        out_shape=jax.ShapeDtypeStruct((depth, rows, n), F32),
        compiler_params=_params("arbitrary", "arbitrary"),
        name="adaln_mod",
    )(cc, ada_w, ada_b.reshape(depth, 1, n))


def _disc_kernel(lr_ref, li_ref, ldt_ref, br_ref, bi_ref, ar_ref, ai_ref, bbr_ref, bbi_ref):
    dt = jnp.exp(ldt_ref[...])
    lr = lr_ref[...]
    li = li_ref[...]
    mag = jnp.exp(dt * lr)
    ab_re = mag * jnp.cos(dt * li)
    ab_im = mag * jnp.sin(dt * li)
    den = lr * lr + li * li
    nr = ab_re - 1.0
    k_re = (nr * lr + ab_im * li) / den
    k_im = (ab_im * lr - nr * li) / den
    br = br_ref[...]
    bi = bi_ref[...]
    ar_ref[...] = ab_re
    ai_ref[...] = ab_im
    bbr_ref[...] = k_re * br - k_im * bi
    bbi_ref[...] = k_re * bi + k_im * br


def _discretise(lam_re, lam_im, log_dt, b_re, b_im):
    lead = lam_re.shape[:-1]
    p = lam_re.shape[-1]
    h = b_re.shape[-1]
    r = 1
    for s in lead:
        r *= s
    lr = lam_re.reshape(r, 1, p)
    li = lam_im.reshape(r, 1, p)
    ldt = jnp.broadcast_to(log_dt.reshape(r, 1, 1), (r, 1, p))
    br = jnp.swapaxes(b_re, -1, -2).reshape(r, h, p)
    bi = jnp.swapaxes(b_im, -1, -2).reshape(r, h, p)
    full = lambda shape: pl.BlockSpec(shape, lambda i: (0,) * len(shape))
    ar, ai, bbr, bbi = pl.pallas_call(
        _disc_kernel,
        grid=(1,),
        in_specs=[full((r, 1, p)), full((r, 1, p)), full((r, 1, p)), full((r, h, p)), full((r, h, p))],
        out_specs=[full((r, 1, p)), full((r, 1, p)), full((r, h, p)), full((r, h, p))],
        out_shape=[jax.ShapeDtypeStruct((r, 1, p), F32), jax.ShapeDtypeStruct((r, 1, p), F32),
                   jax.ShapeDtypeStruct((r, h, p), F32), jax.ShapeDtypeStruct((r, h, p), F32)],
        compiler_params=_params("arbitrary"),
        name="s5_discretise",
    )(lr, li, ldt, br, bi)
    return (ar.reshape(lead + (p,)), ai.reshape(lead + (p,)),
            bbr.reshape(lead + (h, p)), bbi.reshape(lead + (h, p)))


def _block_diag(m):
    g = m.shape[-3]
    eye = jnp.eye(g, dtype=m.dtype)
    out = m[..., :, :, None, :] * eye[:, None, :, None]
    return out.reshape(m.shape[:-3] + (g * m.shape[-2], g * m.shape[-1]))


def _s5_operands(ar, ai, bbr, bbi, c_re, c_im):
    lead = ar.shape[:-2]
    g, p = ar.shape[-2:]
    h = bbr.shape[-2]
    nb = g // S5_BLOCK_GROUPS
    bg = S5_BLOCK_GROUPS
    blk = lambda m: m.astype(BF16).reshape(lead + (nb, bg) + m.shape[-2:])
    b_in = jnp.concatenate([_block_diag(blk(bbr)), _block_diag(blk(bbi))], axis=-1)
    ct = lambda m: jnp.swapaxes(m, -1, -2)
    c_out = jnp.concatenate([_block_diag(blk(ct(c_re))), _block_diag(blk(ct(-c_im)))], axis=-2)
    a_re = ar.reshape(lead + (nb, 1, bg * p))
    a_im = ai.reshape(lead + (nb, 1, bg * p))
    return b_in, c_out, a_re, a_im


def _inproj_kernel(h_ref, mod_ref, g_ref, w_ref, u_ref, a_ref, *, batch, d_s5, d_conv):
    nl = _norm_modulate(h_ref[...], g_ref[...], mod_ref, batch, 0, 1).astype(BF16)
    u_ref[...] = _dot(nl, w_ref[:, :d_s5]).astype(BF16)
    off = d_s5
    v1 = _dot(nl, w_ref[:, off:off + d_conv])
    v2 = _dot(nl, w_ref[:, off + d_conv:off + 2 * d_conv])
    a_ref[...] = (v1 * jax.nn.sigmoid(v2)).astype(BF16)


def _inproj(h, mod, gain, w_in, *, batch, n_ctx_tiles, d_s5, d_conv):
    rows, d = h.shape
    tr = GRID_W * batch
    d_in = w_in.shape[1]
    kind = lambda i: (jnp.where(i >= n_ctx_tiles, 1, 0), 0, 0)
    return pl.pallas_call(
        functools.partial(_inproj_kernel, batch=batch, d_s5=d_s5, d_conv=d_conv),
        grid=(rows // tr,),
        in_specs=[pl.BlockSpec((tr, d), lambda i: (i, 0)),
                  pl.BlockSpec((None, batch, N_MOD * d), kind),
                  pl.BlockSpec((1, d), lambda i: (0, 0)),
                  pl.BlockSpec((d, d_in), lambda i: (0, 0))],
        out_specs=[pl.BlockSpec((tr, d_s5), lambda i: (i, 0)),
                   pl.BlockSpec((tr, d_conv), lambda i: (i, 0))],
        out_shape=[jax.ShapeDtypeStruct((rows, d_s5), BF16),
                   jax.ShapeDtypeStruct((rows, d_conv), BF16)],
        compiler_params=_params("arbitrary"),
        name="mix_inproj",
    )(h, mod, gain, w_in)


def _s5_kernel(uf0_ref, uf1_ref, uf2_ref, ub0_ref, ub1_ref, ub2_ref, bin_ref, cout_ref, are_ref, aim_ref,
               yf_ref, yb_ref, bufs, carry, *, batch, steps):
    m = pl.program_id(1)
    ns = are_ref.shape[-1]
    tr = steps * batch
    u_first = (uf0_ref, ub0_ref)
    u_odd = (uf1_ref, ub1_ref)
    u_next = (uf2_ref, ub2_ref)
    y_refs = (yf_ref, yb_ref)

    def project_in(u_ref, direction, slot):
        bufs[direction, slot] = _dot(u_ref[...], bin_ref[direction])

    def recur(direction, slot):
        a_re = jnp.broadcast_to(are_ref[direction], (batch, ns))
        a_im = jnp.broadcast_to(aim_ref[direction], (batch, ns))
        h_re = carry[direction, :, 0:ns]
        h_im = carry[direction, :, ns:2 * ns]
        for s in range(steps):
            t = (steps - 1 - s) if direction == 1 else s
            rows = slice(t * batch, (t + 1) * batch)
            n_re = a_re * h_re - a_im * h_im + bufs[direction, slot, rows, 0:ns]
            n_im = a_re * h_im + a_im * h_re + bufs[direction, slot, rows, ns:2 * ns]
            bufs[direction, slot, rows, 0:ns] = n_re
            bufs[direction, slot, rows, ns:2 * ns] = n_im
            h_re, h_im = n_re, n_im
        carry[direction, :, 0:ns] = h_re
        carry[direction, :, ns:2 * ns] = h_im

    def project_out(direction, slot, position):
        half = position if direction == 0 else 1 - position
        y = _dot(bufs[direction, slot].astype(BF16), cout_ref[direction])
        y_refs[direction][half * tr:(half + 1) * tr, :] = y.astype(BF16)

    @pl.when(m == 0)
    def _():
        carry[...] = jnp.zeros_like(carry)
        for direction in range(2):
            project_in(u_first[direction], direction, 0)

    for direction in range(2):
        project_in(u_odd[direction], direction, 1)
        recur(direction, 0)
    for direction in range(2):
        project_out(direction, 0, 0)
        recur(direction, 1)
    for direction in range(2):
        project_in(u_next[direction], direction, 0)
        project_out(direction, 1, 1)


def _s5_scan(u, b_in, c_out, a_re, a_im, *, batch, n_ctx_tiles):
    rows, d_s5 = u.shape
    nb = b_in.shape[1]
    ns = a_re.shape[-1]
    tr = S5_STEPS * batch
    nt = rows // tr
    assert nt % 2 == 0 and n_ctx_tiles % 2 == 0

    def rev(p):
        return jnp.where(p < n_ctx_tiles, n_ctx_tiles - 1 - p, nt - 1 + n_ctx_tiles - p)

    nxt = lambda m: jnp.minimum(2 * m + 2, nt - 1)
    chunk = lambda index: pl.BlockSpec((tr, LANES), index)
    param = lambda shape: pl.BlockSpec((2, None) + shape, lambda j, m: (0, j, 0, 0))
    return pl.pallas_call(
        functools.partial(_s5_kernel, batch=batch, steps=S5_STEPS),
        grid=(nb, nt // 2),
        in_specs=[chunk(lambda j, m: (0, j)),
                  chunk(lambda j, m: (2 * m + 1, j)),
                  chunk(lambda j, m: (nxt(m), j)),
                  chunk(lambda j, m: (rev(0), j)),
                  chunk(lambda j, m: (rev(2 * m + 1), j)),
                  chunk(lambda j, m: (rev(nxt(m)), j)),
                  param((LANES, 2 * ns)), param((2 * ns, LANES)), param((1, ns)), param((1, ns))],
        out_specs=[pl.BlockSpec((2 * tr, LANES), lambda j, m: (m, j)),
                   pl.BlockSpec((2 * tr, LANES), lambda j, m: (rev(2 * m + 1) // 2, j))],
        out_shape=[jax.ShapeDtypeStruct((rows, d_s5), BF16), jax.ShapeDtypeStruct((rows, d_s5), BF16)],
        scratch_shapes=[pltpu.VMEM((2, 2, tr, 2 * ns), F32), pltpu.VMEM((2, batch, 2 * ns), F32)],
        compiler_params=_params("arbitrary", "arbitrary"),
        name="s5_scan",
    )(u, u, u, u, u, u, b_in, c_out, a_re, a_im)


def _mixout_kernel(h_ref, mod_ref, g_ref, u_ref, yf_ref, yb_ref, ac_ref, ap_ref, an_ref, wgate_ref,
                   sd_ref, wglu_ref, wa_ref, dw_ref, dwb_ref, lng_ref, lnb_ref, wb_ref, wout_ref,
                   o_ref, aext, conv, *, batch, n_ctx_tiles, n_tiles, first_tile):
    i = pl.program_id(0) + first_tile
    tr, d = h_ref.shape
    halo = ap_ref.shape[0]
    taps = dw_ref.shape[0]

    ys = sd_ref[...] * u_ref[...].astype(F32) + yf_ref[...].astype(F32) + yb_ref[...].astype(F32)
    g = jax.nn.gelu(ys)
    ya = _dot((g * jax.nn.sigmoid(_dot(g.astype(BF16), wglu_ref[...]))).astype(BF16), wa_ref[...])

    nl = _norm_modulate(h_ref[...], g_ref[...], mod_ref, batch, 0, 1).astype(BF16)
    gate_a = jax.nn.sigmoid(_dot(nl, wgate_ref[:, 0:d]))
    gate_b = jax.nn.sigmoid(_dot(nl, wgate_ref[:, d:2 * d]))

    first = jnp.logical_or(i == 0, i == n_ctx_tiles)
    last = jnp.logical_or(i == n_ctx_tiles - 1, i == n_tiles - 1)
    aext[0:halo] = jnp.where(first, 0.0, ap_ref[...].astype(F32))
    aext[halo:halo + tr] = ac_ref[...].astype(F32)
    aext[halo + tr:2 * halo + tr] = jnp.where(last, 0.0, an_ref[...].astype(F32))
    chunk = 2 * batch

    sub = dw_ref.shape[1]
    width = dw_ref.shape[2]
    for c in range(tr // chunk):
        r0 = c * chunk
        acc = jnp.broadcast_to(dwb_ref[...], (chunk, width)).reshape(chunk // sub, sub, width)
        for k in range(taps):
            off = r0 + halo + (k - taps // 2) * batch
            acc = acc + dw_ref[k][None] * aext[off:off + chunk, :].reshape(chunk // sub, sub, width)
        conv[r0:r0 + chunk, :] = acc.reshape(chunk, width)
    a = conv[...]
    mu = jnp.mean(a, axis=-1, keepdims=True)
    ac = a - mu
    var = jnp.mean(ac * ac, axis=-1, keepdims=True)
    ln = (ac * lax.rsqrt(var + EPS)) * lng_ref[...] + lnb_ref[...]
    yb = _dot((ln * jax.nn.sigmoid(ln)).astype(BF16), wb_ref[...])

    merged = gate_a * ya + gate_b * yb
    y = _dot(merged.astype(BF16), wout_ref[...])
    o_ref[...] = _gated_residual(h_ref[...], y, mod_ref, batch, 2)


def _mixout(h, mod, gain, u, yf, yb, a, w_gate, s5_d, w_glu, w_a, conv_dw, conv_dw_b, ln_g, ln_b, w_b, w_out,
            *, batch, n_ctx_tiles, first_tile):
    rows, d = h.shape
    d_s5 = u.shape[1]
    d_conv = a.shape[1]
    tr = MIX_STEPS * batch
    halo = CONV_HALO_STEPS * batch
    ratio = tr // halo
    nt = rows // tr
    n_halo_blocks = rows // halo
    taps = conv_dw.shape[0]
    assert taps // 2 <= CONV_HALO_STEPS
    sublanes = 8
    dw_rows = jnp.broadcast_to(conv_dw[:, None, :], (taps, sublanes, d_conv))
    kind = lambda i: (jnp.where(i + first_tile >= n_ctx_tiles, 1, 0), 0, 0)
    row_tile = lambda w: pl.BlockSpec((tr, w), lambda i: (i + first_tile, 0))
    const = lambda s: pl.BlockSpec(s, lambda i: (0,) * len(s))
    return pl.pallas_call(
        functools.partial(_mixout_kernel, batch=batch, n_ctx_tiles=n_ctx_tiles, n_tiles=nt, first_tile=first_tile),
        grid=(nt - first_tile,),
        in_specs=[row_tile(d),
                  pl.BlockSpec((None, batch, N_MOD * d), kind),
                  const((1, d)),
                  row_tile(d_s5), row_tile(d_s5), row_tile(d_s5),
                  row_tile(d_conv),
                  pl.BlockSpec((halo, d_conv), lambda i: (jnp.maximum((i + first_tile) * ratio - 1, 0), 0)),
                  pl.BlockSpec((halo, d_conv),
                               lambda i: (jnp.minimum((i + first_tile + 1) * ratio, n_halo_blocks - 1), 0)),
                  const((d, 2 * d)),
                  const((1, d_s5)), const((d_s5, d_s5)), const((d_s5, d)),
                  const((taps, sublanes, d_conv)), const((1, d_conv)), const((1, d_conv)), const((1, d_conv)),
                  const((d_conv, d)), const((d, d))],
        out_specs=pl.BlockSpec((tr, d), lambda i: (i, 0)),
        out_shape=jax.ShapeDtypeStruct((rows - first_tile * tr, d), F32),
        scratch_shapes=[pltpu.VMEM((tr + 2 * halo, d_conv), F32), pltpu.VMEM((tr, d_conv), F32)],
        compiler_params=_params("arbitrary"),
        name="mix_out",
    )(h, mod, gain, u, yf, yb, a, a, a, w_gate, s5_d, w_glu, w_a, dw_rows, conv_dw_b, ln_g, ln_b, w_b, w_out)


def _ffn_gate_kernel(h_ref, mod_ref, g_ref, w_ref, o_ref, *, batch):
    nl = _norm_modulate(h_ref[...], g_ref[...], mod_ref, batch, 3, 4).astype(BF16)
    d_ff = o_ref.shape[1]
    for c in range(d_ff // FFN_BLOCK):
        cols = slice(c * FFN_BLOCK, (c + 1) * FFN_BLOCK)
        o_ref[:, cols] = _dot(nl, w_ref[:, cols]).astype(BF16)


def _ffn_gate(h, mod, gain, w_up, *, batch, n_ctx_tiles, d_ff):
    rows, d = h.shape
    tr = GRID_W * batch
    kind = lambda i: (jnp.where(i >= n_ctx_tiles, 1, 0), 0, 0)
    return pl.pallas_call(
        functools.partial(_ffn_gate_kernel, batch=batch),
        grid=(rows // tr,),
        in_specs=[pl.BlockSpec((tr, d), lambda i: (i, 0)),
                  pl.BlockSpec((None, batch, N_MOD * d), kind),
                  pl.BlockSpec((1, d), lambda i: (0, 0)),
                  pl.BlockSpec((d, d_ff), lambda i: (0, 0))],
        out_specs=pl.BlockSpec((tr, d_ff), lambda i: (i, 0)),
        out_shape=jax.ShapeDtypeStruct((rows, d_ff), BF16),
        compiler_params=_params("arbitrary"),
        name="ffn_gate",
    )(h, mod, gain, w_up)


def _ffn_kernel(h_ref, mod_ref, g_ref, gp_ref, gc_ref, gn_ref, wv_ref, wd_ref, dw_ref, db_ref, fg_ref,
                o_ref, nl_ref, act_ref, *, batch, n_ctx_tiles, n_tiles, final_norm):
    i = pl.program_id(0)
    f = pl.program_id(1)
    nf = pl.num_programs(1)
    tr, d = h_ref.shape
    fw = gc_ref.shape[1]
    ksz = 3

    @pl.when(f == 0)
    def _():
        nl_ref[...] = _norm_modulate(h_ref[...], g_ref[...], mod_ref, batch, 3, 4).astype(BF16)
        o_ref[...] = jnp.zeros_like(o_ref)

    is_ctx = i < n_ctx_tiles
    up_ok = jnp.logical_and(jnp.logical_not(is_ctx), i >= n_ctx_tiles + 1)
    down_ok = jnp.logical_and(jnp.logical_not(is_ctx), i <= n_tiles - 2)
    ctx_lo = jnp.logical_and(is_ctx, i >= 1)
    ctx_hi = jnp.logical_and(is_ctx, i <= n_ctx_tiles - 2)
    row_ok = (up_ok, None, down_ok)
    srcs = (gp_ref, gc_ref, gn_ref)
    group = FFN_ROW_BLOCK // batch
    n_blocks = tr // FFN_ROW_BLOCK

    def conv_block(rb):
        r0 = rb * FFN_ROW_BLOCK
        for c in range(fw // LANES):
            cols = slice(c * LANES, (c + 1) * LANES)
            w = []
            for s in range(ksz):
                for k in range(ksz):
                    wk = dw_ref[s * ksz + k:s * ksz + k + 1, cols]
                    if row_ok[s] is not None:
                        wk = jnp.where(row_ok[s], wk, 0.0)
                    w.append(jnp.broadcast_to(wk.astype(BF16), (batch, LANES)))
            bias = jnp.broadcast_to(db_ref[:, cols].astype(BF16), (batch, LANES))

            def step_rows(s, rj):
                zero = jnp.zeros((batch, LANES), BF16)
                if rj < 0:
                    return jnp.where(ctx_lo, gp_ref[tr - batch:tr, cols], zero) if s == 1 else None
                if rj >= tr:
                    return jnp.where(ctx_hi, gn_ref[0:batch, cols], zero) if s == 1 else None
                return srcs[s][rj:rj + batch, cols]

            window = [[step_rows(s, r0 + j * batch) for j in (-1, 0)] for s in range(ksz)]
            for j in range(group):
                rj = r0 + j * batch
                acc = bias
                for s in range(ksz):
                    window[s].append(step_rows(s, rj + batch))
                    for k in range(ksz):
                        if window[s][k] is not None:
                            acc = acc + w[s * ksz + k] * window[s][k]
                    window[s].pop(0)
                act_ref[rj:rj + batch, cols] = acc * jax.nn.sigmoid(acc)

    def project_block(rb):
        rows = slice(rb * FFN_ROW_BLOCK, (rb + 1) * FFN_ROW_BLOCK)
        nl = nl_ref[rows, :]
        for c0 in range(0, fw, MXU_N):
            cols = slice(c0, min(c0 + MXU_N, fw))
            act_ref[rows, cols] = (act_ref[rows, cols].astype(F32) * _dot(nl, wv_ref[:, cols])).astype(BF16)
        o_ref[rows, :] += _dot(act_ref[rows, :], wd_ref[...])

    conv_block(0)
    for rb in range(n_blocks - 1):
        conv_block(rb + 1)
        project_block(rb)
    project_block(n_blocks - 1)

    @pl.when(f == nf - 1)
    def _():
        out = _gated_residual(h_ref[...], o_ref[...], mod_ref, batch, 5)
        if final_norm:
            ms = jnp.mean(out * out, axis=-1, keepdims=True)
            out = (out * lax.rsqrt(ms + EPS)) * fg_ref[...]
        o_ref[...] = out


def _ffn_block(d_ff):
    half = d_ff // 2
    return half if half % LANES == 0 else d_ff


def _ffn(h, mod, gain, gate, w_up, w_down, dw, dw_b, final_gain, *, batch, n_ctx_tiles, d_ff, final_norm):
    rows, d = h.shape
    tr = GRID_W * batch
    nt = rows // tr
    fw = _ffn_block(d_ff)
    nf = d_ff // fw
    kind = lambda i, f: (jnp.where(i >= n_ctx_tiles, 1, 0), 0, 0)
    return pl.pallas_call(
        functools.partial(_ffn_kernel, batch=batch, n_ctx_tiles=n_ctx_tiles, n_tiles=nt, final_norm=final_norm),
        grid=(nt, nf),
        in_specs=[pl.BlockSpec((tr, d), lambda i, f: (i, 0)),
                  pl.BlockSpec((None, batch, N_MOD * d), kind),
                  pl.BlockSpec((1, d), lambda i, f: (0, 0)),
                  pl.BlockSpec((tr, fw), lambda i, f: (jnp.maximum(i - 1, 0), f)),
                  pl.BlockSpec((tr, fw), lambda i, f: (i, f)),
                  pl.BlockSpec((tr, fw), lambda i, f: (jnp.minimum(i + 1, nt - 1), f)),
                  pl.BlockSpec((d, fw), lambda i, f: (0, nf + f)),
                  pl.BlockSpec((fw, d), lambda i, f: (f, 0)),
                  pl.BlockSpec((9, fw), lambda i, f: (0, f)),
                  pl.BlockSpec((1, fw), lambda i, f: (0, f)),
                  pl.BlockSpec((1, d), lambda i, f: (0, 0))],
        out_specs=pl.BlockSpec((tr, d), lambda i, f: (i, 0)),
        out_shape=jax.ShapeDtypeStruct((rows, d), F32),
        scratch_shapes=[pltpu.VMEM((tr, d), BF16), pltpu.VMEM((tr, fw), BF16)],
        compiler_params=pltpu.CompilerParams(dimension_semantics=("arbitrary", "arbitrary"),
                                             vmem_limit_bytes=FFN_VMEM_LIMIT),
        name="ffn_main",
    )(h, mod, gain, gate, gate, gate, w_up, w_down, dw, dw_b, final_gain)


def kernel(x, c, ctx, c_ctx, ada_w, ada_b, norm1_g, w_in, s5_lam_re, s5_lam_im, s5_log_dt, s5_b_re, s5_b_im, s5_c_re, s5_c_im, s5_d, w_glu, w_a, conv_dw, conv_dw_b, conv_ln_g, conv_ln_b, w_b, w_out, norm2_g, ffn_w_up, ffn_dw, ffn_dw_b, ffn_w_down, final_g):
    batch, seq, d = x.shape
    ctx_len = ctx.shape[1]
    depth = ada_w.shape[0]
    d_s5 = w_glu.shape[1]
    d_conv = conv_dw.shape[2]
    d_ff = ffn_dw.shape[-1]
    assert batch % 16 == 0 and seq % GRID_W == 0 and ctx_len % GRID_W == 0
    assert d_s5 % LANES == 0 and d_ff % FFN_BLOCK == 0
    mix_ctx_tiles = ctx_len // MIX_STEPS
    ffn_ctx_tiles = ctx_len // GRID_W

    h = jnp.concatenate([jnp.swapaxes(ctx, 0, 1), jnp.swapaxes(x, 0, 1)], axis=0).reshape(-1, d)

    cc = jnp.concatenate([jnp.broadcast_to(c_ctx[None], (batch, d)), c], axis=0)
    mods = _modulation(cc, ada_w, ada_b).reshape(depth, 2, batch, N_MOD * d)

    ar, ai, bbr, bbi = _discretise(s5_lam_re, s5_lam_im, s5_log_dt, s5_b_re, s5_b_im)
    b_in, c_out, a_re, a_im = _s5_operands(ar, ai, bbr, bbi, s5_c_re, s5_c_im)

    row = lambda v: v.reshape(1, -1)
    for i in range(depth):
        mod = mods[i]
        off_gate = d_s5 + 2 * d_conv
        u, a = _inproj(h, mod, row(norm1_g[i]), w_in[i, :, :off_gate].astype(BF16), batch=batch,
                       n_ctx_tiles=ffn_ctx_tiles, d_s5=d_s5, d_conv=d_conv)
        yf, yb = _s5_scan(u, b_in[i], c_out[i], a_re[i], a_im[i], batch=batch, n_ctx_tiles=ctx_len // S5_STEPS)
        last = i == depth - 1
        h = _mixout(h, mod, row(norm1_g[i]), u, yf, yb, a, w_in[i, :, off_gate:].astype(BF16),
                    row(s5_d[i]), w_glu[i].astype(BF16), w_a[i].astype(BF16),
                    conv_dw[i], row(conv_dw_b[i]), row(conv_ln_g[i]), row(conv_ln_b[i]),
                    w_b[i].astype(BF16), w_out[i].astype(BF16), batch=batch, n_ctx_tiles=mix_ctx_tiles,
                    first_tile=mix_ctx_tiles if last else 0)
        ctx_tiles = 0 if last else ffn_ctx_tiles
        w_up = ffn_w_up[i].astype(BF16)
        gate = _ffn_gate(h, mod, row(norm2_g[i]), w_up, batch=batch, n_ctx_tiles=ctx_tiles, d_ff=d_ff)
        h = _ffn(h, mod, row(norm2_g[i]), gate, w_up, ffn_w_down[i].astype(BF16),
                 ffn_dw[i].reshape(9, d_ff), row(ffn_dw_b[i]), row(final_g), batch=batch, n_ctx_tiles=ctx_tiles,
                 d_ff=d_ff, final_norm=last)

    return jnp.swapaxes(h.reshape(seq, batch, d), 0, 1)
```
